```python
import math
import jax, jax.numpy as jnp
from jax import lax
import numpy as np

D_MODEL = 2048
BATCH = 8
SEQ = 2048
DEPTH = 2

CHUNK = 64
Q_BLOCK = 128
HEAD_DIM = 128
N_RET_HEADS = D_MODEL // (2 * HEAD_DIM)
N_FOX_HEADS = D_MODEL // (2 * HEAD_DIM)
N_DIFF_HEADS = D_MODEL // (2 * HEAD_DIM)
RET_W = N_RET_HEADS * HEAD_DIM
FOX_W = N_FOX_HEADS * HEAD_DIM
EVEN_IN = 4 * RET_W + 3 * FOX_W + N_FOX_HEADS
DIFF_QK = N_DIFF_HEADS * 2 * HEAD_DIM
DIFF_V = 2 * HEAD_DIM
ODD_IN = 2 * DIFF_QK + N_DIFF_HEADS * DIFF_V
D_FF = 4 * D_MODEL
ROPE_BASE = 10000.0
RMS_EPS = 1e-6
GN_EPS = 1e-5
FORGET_BIAS_INIT = 3.0
N_EVEN = (DEPTH + 1) // 2
N_ODD = DEPTH // 2

kernel_name = "chunk_causal_retention_fox_diffattn_hybrid"


def rms_norm(x, g):
    xf = x.astype(jnp.float32)
    y = xf * lax.rsqrt(jnp.mean(xf * xf, axis=-1, keepdims=True) + RMS_EPS)
    return (y * g.astype(jnp.float32)).astype(x.dtype)


def head_rms_norm(x, g):
    y = x * lax.rsqrt(jnp.mean(x * x, axis=-1, keepdims=True) + GN_EPS)
    return y * g.astype(jnp.float32)


def rotary(x, pos):
    half = x.shape[-1] // 2
    inv = ROPE_BASE ** (-jnp.arange(half, dtype=jnp.float32) / half)
    ang = pos.astype(jnp.float32)[:, None] * inv[None, :]
    cos = jnp.cos(ang)[None, :, None, :]
    sin = jnp.sin(ang)[None, :, None, :]
    xf = x.astype(jnp.float32)
    x1, x2 = xf[..., :half], xf[..., half:]
    return jnp.concatenate([x1 * cos - x2 * sin, x2 * cos + x1 * sin], axis=-1)


def retention_chunkwise(q, k, v):
    B, S, H, Dh = q.shape
    n = S // CHUNK
    log_g = jnp.log1p(-(2.0 ** (-5.0 - jnp.arange(H, dtype=jnp.float32))))
    q = q.astype(jnp.float32).reshape(B, n, CHUNK, H, Dh)
    k = (k.astype(jnp.float32) * (Dh ** -0.5)).reshape(B, n, CHUNK, H, Dh)
    v = v.astype(jnp.float32).reshape(B, n, CHUNK, H, Dh)
    idx = jnp.arange(CHUNK, dtype=jnp.float32)
    d_intra = jnp.exp(log_g[:, None, None] * jnp.abs(idx[:, None] - idx[None, :]))
    scores = jnp.einsum('bnihd,bnjhd->bnhij', q, k) * d_intra
    intra = jnp.einsum('bnhij,bnjhd->bnihd', scores, v)
    zeta = jnp.exp(log_g[None, :] * (CHUNK - 1.0 - idx)[:, None])
    kv = jnp.einsum('bnjhd,jh,bnjhe->nbhde', k, zeta, v)
    g_chunk = jnp.exp(log_g * CHUNK)[None, :, None, None]

    def step(state, kv_c):
        return state * g_chunk + kv_c, state

    _, prev = lax.scan(step, jnp.zeros((B, H, Dh, Dh), jnp.float32), kv)
    xi = jnp.exp(log_g[None, :] * (idx + 1.0)[:, None])
    cross = jnp.einsum('bnihd,nbhde,ih->bnihe', q, prev, xi)
    return (intra + cross).reshape(B, S, H, Dh)


def forgetting_attention(q, k, v, log_f):
    B, S, H, Dh = q.shape
    F = jnp.cumsum(log_f, axis=1).transpose(0, 2, 1)
    scale = Dh ** -0.5

    def block(b):
        q0, q1 = b * Q_BLOCK, (b + 1) * Q_BLOCK
        s = jnp.einsum('bqhd,bkhd->bhqk', q[:, q0:q1], k[:, :q1]).astype(jnp.float32) * scale
        bias = F[:, :, q0:q1, None] - F[:, :, None, :q1]
        mask = jnp.arange(q1)[None, :] <= jnp.arange(q0, q1)[:, None]
        p = jax.nn.softmax(jnp.where(mask, s + bias, -jnp.inf), axis=-1)
        return jnp.einsum('bhqk,bkhd->bqhd', p.astype(v.dtype), v[:, :q1])

    return jnp.concatenate([block(b) for b in range(S // Q_BLOCK)], axis=1)


def differential_attention(q, k, v, lam):
    B, S, H, _, Dh = q.shape
    scale = Dh ** -0.5

    def block(b):
        q0, q1 = b * Q_BLOCK, (b + 1) * Q_BLOCK
        s = jnp.einsum('bqhcd,bkhcd->bhcqk', q[:, q0:q1], k[:, :q1]).astype(jnp.float32) * scale
        mask = (jnp.arange(q1) // CHUNK)[None, :] <= (jnp.arange(q0, q1) // CHUNK)[:, None]
        p = jax.nn.softmax(jnp.where(mask, s, -jnp.inf), axis=-1)
        a = p[:, :, 0] - lam * p[:, :, 1]
        return jnp.einsum('bhqk,bkhe->bqhe', a, v[:, :q1].astype(jnp.float32))

    return jnp.concatenate([block(b) for b in range(S // Q_BLOCK)], axis=1)


def even_mixer(h, w_in, b_f, ret_gn, w_out, pos):
    B, S, _ = h.shape
    z = h @ w_in
    cuts = [RET_W, 2 * RET_W, 3 * RET_W, 4 * RET_W,
            4 * RET_W + FOX_W, 4 * RET_W + 2 * FOX_W, 4 * RET_W + 3 * FOX_W]
    rq, rk, rv, rg, fq, fk, fv, ff = jnp.split(z, cuts, axis=-1)
    rh = lambda t: t.reshape(B, S, N_RET_HEADS, HEAD_DIM)
    fh = lambda t: t.reshape(B, S, N_FOX_HEADS, HEAD_DIM)
    ret = retention_chunkwise(rotary(rh(rq), pos), rotary(rh(rk), pos), rh(rv))
    ret = head_rms_norm(ret, ret_gn.reshape(N_RET_HEADS, HEAD_DIM)).reshape(B, S, RET_W)
    ret = (jax.nn.silu(rg.astype(jnp.float32)) * ret).astype(h.dtype)
    log_f = jax.nn.log_sigmoid((ff + b_f).astype(jnp.float32))
    fox = forgetting_attention(fh(fq), fh(fk), fh(fv), log_f).reshape(B, S, FOX_W)
    return jnp.concatenate([ret, fox.astype(h.dtype)], axis=-1) @ w_out


def odd_mixer(h, w_in, lq1, lk1, lq2, lk2, subln_g, w_out, lambda_init):
    B, S, _ = h.shape
    z = h @ w_in
    q = z[..., :DIFF_QK].reshape(B, S, N_DIFF_HEADS, 2, HEAD_DIM)
    k = z[..., DIFF_QK:2 * DIFF_QK].reshape(B, S, N_DIFF_HEADS, 2, HEAD_DIM)
    v = z[..., 2 * DIFF_QK:].reshape(B, S, N_DIFF_HEADS, DIFF_V)
    f32 = jnp.float32
    lam = (jnp.exp(jnp.sum(lq1.astype(f32) * lk1.astype(f32)))
           - jnp.exp(jnp.sum(lq2.astype(f32) * lk2.astype(f32))) + lambda_init)
    o = differential_attention(q, k, v, lam)
    o = head_rms_norm(o, subln_g[None, :]) * (1.0 - lambda_init)
    return o.reshape(B, S, N_DIFF_HEADS * DIFF_V).astype(h.dtype) @ w_out


def squared_relu_mlp(h, w1, w2):
    a = jax.nn.relu(h @ w1)
    return (a * a) @ w2


def setup_inputs(seed: int = 0) -> dict:
    key = jax.random.key(seed)
    ks = jax.random.split(key, 20)
    nrm = lambda k, shape, s: jax.random.normal(k, shape, jnp.float32) * s
    return {
        "x": nrm(ks[0], (BATCH, SEQ, D_MODEL), 1.0),
        "norm_mix_g": 1.0 + nrm(ks[1], (DEPTH, D_MODEL), 0.02),
        "norm_mlp_g": 1.0 + nrm(ks[2], (DEPTH, D_MODEL), 0.02),
        "even_w_in": nrm(ks[3], (N_EVEN, D_MODEL, EVEN_IN), D_MODEL ** -0.5),
        "even_b_f": FORGET_BIAS_INIT + nrm(ks[4], (N_EVEN, N_FOX_HEADS), 0.1),
        "even_ret_gn": 1.0 + nrm(ks[5], (N_EVEN, RET_W), 0.02),
        "even_w_out": nrm(ks[6], (N_EVEN, RET_W + FOX_W, D_MODEL), (RET_W + FOX_W) ** -0.5),
        "odd_w_in": nrm(ks[7], (N_ODD, D_MODEL, ODD_IN), D_MODEL ** -0.5),
        "odd_lambda_q1": nrm(ks[8], (N_ODD, HEAD_DIM), 0.1),
        "odd_lambda_k1": nrm(ks[9], (N_ODD, HEAD_DIM), 0.1),
        "odd_lambda_q2": nrm(ks[10], (N_ODD, HEAD_DIM), 0.1),
        "odd_lambda_k2": nrm(ks[11], (N_ODD, HEAD_DIM), 0.1),
        "odd_subln_g": 1.0 + nrm(ks[12], (N_ODD, DIFF_V), 0.02),
        "odd_w_out": nrm(ks[13], (N_ODD, N_DIFF_HEADS * DIFF_V, D_MODEL), (N_DIFF_HEADS * DIFF_V) ** -0.5),
        "mlp_w1": nrm(ks[14], (DEPTH, D_MODEL, D_FF), D_MODEL ** -0.5),
        "mlp_w2": nrm(ks[15], (DEPTH, D_FF, D_MODEL), D_FF ** -0.5),
        "final_g": 1.0 + nrm(ks[16], (D_MODEL,), 0.02),
    }


def reference(x, norm_mix_g, norm_mlp_g, even_w_in, even_b_f, even_ret_gn, even_w_out,
              odd_w_in, odd_lambda_q1, odd_lambda_k1, odd_lambda_q2, odd_lambda_k2,
              odd_subln_g, odd_w_out, mlp_w1, mlp_w2, final_g):
    pos = jnp.arange(x.shape[1], dtype=jnp.int32)
    h = x
    for i in range(DEPTH):
        j = i // 2
        hn = rms_norm(h, norm_mix_g[i])
        if i % 2 == 0:
            mix = even_mixer(hn, even_w_in[j], even_b_f[j], even_ret_gn[j], even_w_out[j], pos)
        else:
            lambda_init = 0.8 - 0.6 * math.exp(-0.3 * i)
            mix = odd_mixer(hn, odd_w_in[j], odd_lambda_q1[j], odd_lambda_k1[j],
                            odd_lambda_q2[j], odd_lambda_k2[j], odd_subln_g[j],
                            odd_w_out[j], lambda_init)
        h = h + mix.astype(h.dtype)
        h = h + squared_relu_mlp(rms_norm(h, norm_mlp_g[i]), mlp_w1[i], mlp_w2[i]).astype(h.dtype)
    return rms_norm(h, final_g)
```

```python
import functools
import math

import jax
import jax.numpy as jnp
from jax import lax
from jax.experimental import pallas as pl
from jax.experimental.pallas import tpu as pltpu

F32 = jnp.float32
BF16 = jnp.bfloat16

CHUNK = 64
CHUNK_SHIFT = 6
HEAD_DIM = 128
N_HEADS = 8
ROPE_BASE = 10000.0
RMS_EPS = 1e-6
GN_EPS = 1e-5
LANES = 128
MIB = 1024 * 1024

PROJ_TM = 1024
PROJ_TN = 1024
MLP_TM = 512
MLP_TF = 1024
RET_ROWS = 256
ATT_TQ = 512
CUMSUM_ROWS = 256


def _params(semantics, vmem_mib):
    return pltpu.CompilerParams(dimension_semantics=semantics,
                                vmem_limit_bytes=vmem_mib * MIB)


def _dot(a, b):
    return jnp.dot(a, b, preferred_element_type=F32)


def _dot_nt(a, b):
    return lax.dot_general(a, b, (((1,), (1,)), ((), ())), preferred_element_type=F32)


def _dot_tn(a, b):
    return lax.dot_general(a, b, (((0,), (0,)), ((), ())), preferred_element_type=F32)


def _rms_normalize(x, g):
    ms = jnp.mean(x * x, axis=-1, keepdims=True)
    return (x * lax.rsqrt(ms + RMS_EPS)) * g


def _norm_proj_kernel(x_ref, g_ref, w_ref, *rest, with_forget):
    if with_forget:
        wf_ref, o_ref, f_ref, xn_ref = rest
    else:
        o_ref, xn_ref = rest

    @pl.when(pl.program_id(1) == 0)
    def _():
        xn_ref[...] = _rms_normalize(x_ref[...], g_ref[...]).astype(BF16)
        if with_forget:
            f_ref[...] = _dot(xn_ref[...], wf_ref[...])

    o_ref[...] = _dot(xn_ref[...], w_ref[...]).astype(o_ref.dtype)


def _norm_proj(x, g, w, wf=None):
    t, d = x.shape
    n = w.shape[1]
    tm, tn = PROJ_TM, PROJ_TN
    with_forget = wf is not None
    in_specs = [
        pl.BlockSpec((tm, d), lambda i, j: (i, 0)),
        pl.BlockSpec((1, d), lambda i, j: (0, 0)),
        pl.BlockSpec((d, tn), lambda i, j: (0, j)),
    ]
    out_shape = [jax.ShapeDtypeStruct((t, n), BF16)]
    out_specs = [pl.BlockSpec((tm, tn), lambda i, j: (i, j))]
    args = [x, g, w]
    if with_forget:
        in_specs.append(pl.BlockSpec((d, LANES), lambda i, j: (0, 0)))
        out_shape.append(jax.ShapeDtypeStruct((t, LANES), F32))
        out_specs.append(pl.BlockSpec((tm, LANES), lambda i, j: (i, 0)))
        args.append(wf)
    out = pl.pallas_call(
        functools.partial(_norm_proj_kernel, with_forget=with_forget),
        grid=(t // tm, n // tn),
        in_specs=in_specs,
        out_specs=out_specs,
        out_shape=out_shape,
        scratch_shapes=[pltpu.VMEM((tm, d), BF16)],
        compiler_params=_params(("parallel", "arbitrary"), 48),
        name="norm_proj_forget" if with_forget else "norm_proj",
    )(*args)
    return out if with_forget else out[0]


def _rotate(x, cos, sin_signed):
    return x * cos + pltpu.roll(x, HEAD_DIM // 2, 1) * sin_signed


def _retention_kernel(lg_ref, q_ref, k_ref, v_ref, gate_ref, cos_ref, sin_ref, gn_ref,
                      o_ref, state_ref):
    rows = q_ref.shape[0]
    h = pl.program_id(1)

    @pl.when(pl.program_id(2) == 0)
    def _():
        state_ref[...] = jnp.zeros_like(state_ref)

    lg = lg_ref[h]
    cos = cos_ref[...]
    sin = sin_ref[...]
    q = _rotate(q_ref[...].astype(F32), cos, sin) * (HEAD_DIM ** -0.5)
    k = _rotate(k_ref[...].astype(F32), cos, sin)
    v = v_ref[...]

    row = lax.broadcasted_iota(jnp.int32, (rows, 1), 0).astype(F32)
    xi = jnp.exp(lg * (row + 1.0))
    zeta = jnp.exp(lg * (rows - 1.0 - row))
    ti = lax.broadcasted_iota(jnp.int32, (rows, rows), 0)
    si = lax.broadcasted_iota(jnp.int32, (rows, rows), 1)
    visible = (si >> CHUNK_SHIFT) <= (ti >> CHUNK_SHIFT)
    decay = jnp.where(visible, jnp.exp(lg * jnp.abs(ti - si).astype(F32)), 0.0)

    state = state_ref[...]
    scores = _dot_nt(q.astype(BF16), k.astype(BF16)) * decay
    out = _dot(scores.astype(BF16), v) + _dot((q * xi).astype(BF16), state.astype(BF16))
    block_decay = jnp.exp(jnp.full((1, HEAD_DIM), lg * rows, F32))
    state_ref[...] = state * block_decay + _dot_tn((k * zeta).astype(BF16), v)

    ms = jnp.mean(out * out, axis=-1, keepdims=True)
    y = out * lax.rsqrt(ms + GN_EPS) * gn_ref[...]
    gate = gate_ref[...].astype(F32)
    o_ref[...] = (jax.nn.silu(gate) * y).astype(o_ref.dtype)


def _retention(z, log_g, cos, sin_signed, gn, batch, seq):
    t = z.shape[0]
    rows = RET_ROWS
    nblk = seq // rows
    col = lambda off: pl.BlockSpec((rows, HEAD_DIM), lambda b, h, r: (b * nblk + r, off + h))
    tab = pl.BlockSpec((rows, HEAD_DIM), lambda b, h, r: (r, 0))
    return pl.pallas_call(
        _retention_kernel,
        grid=(batch, N_HEADS, nblk),
        in_specs=[
            pl.BlockSpec(memory_space=pltpu.SMEM),
            col(0), col(N_HEADS), col(2 * N_HEADS), col(3 * N_HEADS),
            tab, tab,
            pl.BlockSpec((1, HEAD_DIM), lambda b, h, r: (0, h)),
        ],
        out_specs=pl.BlockSpec((rows, HEAD_DIM), lambda b, h, r: (b * nblk + r, h)),
        out_shape=jax.ShapeDtypeStruct((t, N_HEADS * HEAD_DIM), BF16),
        scratch_shapes=[pltpu.VMEM((HEAD_DIM, HEAD_DIM), F32)],
        compiler_params=_params(("parallel", "parallel", "arbitrary"), 32),
        name="retention",
    )(log_g, z, z, z, z, cos, sin_signed, gn)


def _forget_cumsum_kernel(ff_ref, b_ref, tri_ref, o_ref):
    seq = ff_ref.shape[0]
    rows = tri_ref.shape[0]
    carry = jnp.zeros((1, LANES), F32)
    for blk in range(seq // rows):
        sl = slice(blk * rows, (blk + 1) * rows)
        log_f = jax.nn.log_sigmoid(ff_ref[sl, :] + b_ref[...])
        c = jnp.dot(tri_ref[...], log_f, precision=lax.Precision.HIGHEST,
                    preferred_element_type=F32) + carry
        carry = c[rows - 1:rows, :]
        o_ref[:, sl] = c.T[:N_HEADS, :]


def _forget_cumsum(ff, b_f, batch, seq):
    rows = CUMSUM_ROWS
    tri = (jnp.arange(rows)[None, :] <= jnp.arange(rows)[:, None]).astype(F32)
    return pl.pallas_call(
        _forget_cumsum_kernel,
        grid=(batch,),
        in_specs=[
            pl.BlockSpec((seq, LANES), lambda b: (b, 0)),
            pl.BlockSpec((1, LANES), lambda b: (0, 0)),
            pl.BlockSpec((rows, rows), lambda b: (0, 0)),
        ],
        out_specs=pl.BlockSpec((None, N_HEADS, seq), lambda b: (b, 0, 0)),
        out_shape=jax.ShapeDtypeStruct((batch, N_HEADS, seq), F32),
        compiler_params=_params(("parallel",), 32),
        name="forget_cumsum",
    )(ff, b_f, tri)


def _softmax_step(s, v, m_ref, l_ref, acc_ref):
    m_prev = m_ref[...]
    m_new = jnp.maximum(m_prev, jnp.max(s, axis=-1, keepdims=True))
    alpha = jnp.exp(m_prev - m_new)
    p = jnp.exp(s - m_new)
    l_ref[...] = alpha * l_ref[...] + jnp.sum(p, axis=-1, keepdims=True)
    acc_ref[...] = alpha * acc_ref[...] + _dot(p.astype(BF16), v)
    m_ref[...] = m_new


def _fox_kernel(q_ref, k_ref, v_ref, f_ref, o_ref, m_ref, l_ref, acc_ref):
    tq = q_ref.shape[0]
    i = pl.program_id(2)
    q = q_ref[...]
    m_ref[...] = jnp.full_like(m_ref, -jnp.inf)
    l_ref[...] = jnp.zeros_like(l_ref)
    acc_ref[...] = jnp.zeros_like(acc_ref)
    scale = HEAD_DIM ** -0.5

    def scores(j):
        start = pl.multiple_of(j * tq, tq)
        k = k_ref[pl.ds(start, tq), :]
        v = v_ref[pl.ds(start, tq), :]
        return _dot_nt(q, k) * scale - f_ref[j], v

    def body(j, carry):
        s, v = scores(j)
        _softmax_step(s, v, m_ref, l_ref, acc_ref)
        return carry

    lax.fori_loop(0, i, body, 0)

    s, v = scores(i)
    ti = lax.broadcasted_iota(jnp.int32, (tq, tq), 0)
    si = lax.broadcasted_iota(jnp.int32, (tq, tq), 1)
    s = jnp.where(si <= ti, s, -jnp.inf)
    _softmax_step(s, v, m_ref, l_ref, acc_ref)
    o_ref[...] = (acc_ref[...] / l_ref[...]).astype(o_ref.dtype)


def _fox_attention(z, fcum, batch, seq):
    t = z.shape[0]
    tq = ATT_TQ
    nq = seq // tq
    f = fcum.reshape(batch * N_HEADS, nq, 1, tq)
    whole = lambda off: pl.BlockSpec((seq, HEAD_DIM), lambda b, h, i: (b, off + h))
    return pl.pallas_call(
        _fox_kernel,
        grid=(batch, N_HEADS, nq),
        in_specs=[
            pl.BlockSpec((tq, HEAD_DIM), lambda b, h, i: (b * nq + i, 4 * N_HEADS + h)),
            whole(5 * N_HEADS), whole(6 * N_HEADS),
            pl.BlockSpec((None, nq, 1, tq), lambda b, h, i: (b * N_HEADS + h, 0, 0, 0)),
        ],
        out_specs=pl.BlockSpec((tq, HEAD_DIM), lambda b, h, i: (b * nq + i, h)),
        out_shape=jax.ShapeDtypeStruct((t, N_HEADS * HEAD_DIM), BF16),
        scratch_shapes=[pltpu.VMEM((tq, 1), F32), pltpu.VMEM((tq, 1), F32),
                        pltpu.VMEM((tq, HEAD_DIM), F32)],
        compiler_params=_params(("parallel", "parallel", "arbitrary"), 32),
        name="fox_attention",
    )(z, z, z, f)


def _diff_kernel(q_ref, k_ref, v_ref, lq1_ref, lk1_ref, lq2_ref, lk2_ref, g_ref, o_ref,
                 m1_ref, l1_ref, a1_ref, m2_ref, l2_ref, a2_ref, *, lambda_init):
    tq = q_ref.shape[0]
    i = pl.program_id(2)
    scale = HEAD_DIM ** -0.5
    q1 = q_ref[:, :HEAD_DIM]
    q2 = q_ref[:, HEAD_DIM:]
    for m_ref, l_ref, a_ref in ((m1_ref, l1_ref, a1_ref), (m2_ref, l2_ref, a2_ref)):
        m_ref[...] = jnp.full_like(m_ref, -jnp.inf)
        l_ref[...] = jnp.zeros_like(l_ref)
        a_ref[...] = jnp.zeros_like(a_ref)

    def step(j, masked):
        start = pl.multiple_of(j * tq, tq)
        k1 = k_ref[pl.ds(start, tq), :HEAD_DIM]
        k2 = k_ref[pl.ds(start, tq), HEAD_DIM:]
        v = v_ref[pl.ds(start, tq), :]
        s1 = _dot_nt(q1, k1) * scale
        s2 = _dot_nt(q2, k2) * scale
        if masked:
            ti = lax.broadcasted_iota(jnp.int32, (tq, tq), 0)
            si = lax.broadcasted_iota(jnp.int32, (tq, tq), 1)
            visible = (si >> CHUNK_SHIFT) <= (ti >> CHUNK_SHIFT)
            s1 = jnp.where(visible, s1, -jnp.inf)
            s2 = jnp.where(visible, s2, -jnp.inf)
        _softmax_step(s1, v, m1_ref, l1_ref, a1_ref)
        _softmax_step(s2, v, m2_ref, l2_ref, a2_ref)

    def body(j, carry):
        step(j, False)
        return carry

    lax.fori_loop(0, i, body, 0)
    step(i, True)

    lam = (jnp.exp(jnp.sum(lq1_ref[...] * lk1_ref[...], axis=-1, keepdims=True))
           - jnp.exp(jnp.sum(lq2_ref[...] * lk2_ref[...], axis=-1, keepdims=True))
           + lambda_init)
    o = a1_ref[...] / l1_ref[...] - lam * (a2_ref[...] / l2_ref[...])
    ms = jnp.mean(o * o, axis=-1, keepdims=True)
    y = o * lax.rsqrt(ms + GN_EPS) * g_ref[...] * (1.0 - lambda_init)
    o_ref[...] = y.astype(o_ref.dtype)


def _diff_attention(z, lq1, lk1, lq2, lk2, subln_g, lambda_init, batch, seq):
    t = z.shape[0]
    tq = ATT_TQ
    nq = seq // tq
    width = 2 * HEAD_DIM
    whole = lambda off: pl.BlockSpec((seq, width), lambda b, h, i: (b, off + h))
    vec = pl.BlockSpec((1, HEAD_DIM), lambda b, h, i: (0, 0))
    return pl.pallas_call(
        functools.partial(_diff_kernel, lambda_init=lambda_init),
        grid=(batch, N_HEADS, nq),
        in_specs=[
            pl.BlockSpec((tq, width), lambda b, h, i: (b * nq + i, h)),
            whole(N_HEADS), whole(2 * N_HEADS),
            vec, vec, vec, vec,
            pl.BlockSpec((1, width), lambda b, h, i: (0, 0)),
        ],
        out_specs=pl.BlockSpec((tq, width), lambda b, h, i: (b * nq + i, h)),
        out_shape=jax.ShapeDtypeStruct((t, N_HEADS * width), BF16),
        scratch_shapes=[pltpu.VMEM((tq, 1), F32), pltpu.VMEM((tq, 1), F32),
                        pltpu.VMEM((tq, width), F32),
                        pltpu.VMEM((tq, 1), F32), pltpu.VMEM((tq, 1), F32),
                        pltpu.VMEM((tq, width), F32)],
        compiler_params=_params(("parallel", "parallel", "arbitrary"), 32),
        name="diff_attention",
    )(z, z, z, lq1, lk1, lq2, lk2, subln_g)


def _proj_residual_kernel(*refs, n_parts):
    a_refs = refs[:n_parts]
    w_refs = refs[n_parts:2 * n_parts]
    res_ref, o_ref = refs[2 * n_parts:]
    acc = res_ref[...]
    for a_ref, w_ref in zip(a_refs, w_refs):
        acc = acc + _dot(a_ref[...], w_ref[...])
    o_ref[...] = acc


def _proj_residual(parts, w, res):
    t, n = res.shape
    tm, tn = PROJ_TM, PROJ_TN
    kp = parts[0].shape[1]
    n_parts = len(parts)
    a_specs = [pl.BlockSpec((tm, kp), lambda i, j: (i, 0)) for _ in parts]
    w_specs = [pl.BlockSpec((kp, tn), functools.partial(lambda i, j, p: (p, j), p=p))
               for p in range(n_parts)]
    return pl.pallas_call(
        functools.partial(_proj_residual_kernel, n_parts=n_parts),
        grid=(t // tm, n // tn),
        in_specs=a_specs + w_specs + [pl.BlockSpec((tm, tn), lambda i, j: (i, j))],
        out_specs=pl.BlockSpec((tm, tn), lambda i, j: (i, j)),
        out_shape=jax.ShapeDtypeStruct((t, n), F32),
        compiler_params=_params(("parallel", "parallel"), 48),
        name="proj_residual",
    )(*parts, *([w] * n_parts), res)


def _mlp_kernel(h_ref, g_ref, w1_ref, w2_ref, *rest, with_final):
    if with_final:
        fg_ref, o_ref, xn_ref, acc_ref = rest
    else:
        o_ref, xn_ref, acc_ref = rest
    f = pl.program_id(1)

    @pl.when(f == 0)
    def _():
        xn_ref[...] = _rms_normalize(h_ref[...], g_ref[...]).astype(BF16)
        acc_ref[...] = jnp.zeros_like(acc_ref)

    a = jnp.maximum(_dot(xn_ref[...], w1_ref[...]), 0.0)
    acc_ref[...] += _dot((a * a).astype(BF16), w2_ref[...])

    @pl.when(f == pl.num_programs(1) - 1)
    def _():
        out = h_ref[...] + acc_ref[...]
        if with_final:
            out = _rms_normalize(out, fg_ref[...])
        o_ref[...] = out


def _mlp(h, g, w1, w2, final_g=None):
    t, d = h.shape
    dff = w1.shape[1]
    tm, tf = MLP_TM, MLP_TF
    with_final = final_g is not None
    in_specs = [
        pl.BlockSpec((tm, d), lambda i, f: (i, 0)),
        pl.BlockSpec((1, d), lambda i, f: (0, 0)),
        pl.BlockSpec((d, tf), lambda i, f: (0, f)),
        pl.BlockSpec((tf, d), lambda i, f: (f, 0)),
    ]
    args = [h, g, w1, w2]
    if with_final:
        in_specs.append(pl.BlockSpec((1, d), lambda i, f: (0, 0)))
        args.append(final_g)
    return pl.pallas_call(
        functools.partial(_mlp_kernel, with_final=with_final),
        grid=(t // tm, dff // tf),
        in_specs=in_specs,
        out_specs=pl.BlockSpec((tm, d), lambda i, f: (i, 0)),
        out_shape=jax.ShapeDtypeStruct((t, d), F32),
        scratch_shapes=[pltpu.VMEM((tm, d), BF16), pltpu.VMEM((tm, d), F32)],
        compiler_params=_params(("parallel", "arbitrary"), 56),
        name="mlp_final" if with_final else "mlp",
    )(*args)


def kernel(x, norm_mix_g, norm_mlp_g, even_w_in, even_b_f, even_ret_gn, even_w_out,
           odd_w_in, odd_lambda_q1, odd_lambda_k1, odd_lambda_q2, odd_lambda_k2,
           odd_subln_g, odd_w_out, mlp_w1, mlp_w2, final_g):
    batch, seq, d = x.shape
    t = batch * seq
    ret_w = N_HEADS * HEAD_DIM
    row = lambda v: v.reshape(1, -1).astype(F32)

    half = HEAD_DIM // 2
    inv = ROPE_BASE ** (-jnp.arange(half, dtype=F32) / half)
    ang = jnp.arange(seq, dtype=F32)[:, None] * inv[None, :]
    cos = jnp.concatenate([jnp.cos(ang), jnp.cos(ang)], axis=-1)
    sin_signed = jnp.concatenate([-jnp.sin(ang), jnp.sin(ang)], axis=-1)
    log_g = jnp.log1p(-(2.0 ** (-5.0 - jnp.arange(N_HEADS, dtype=F32))))

    h = x.reshape(t, d)

    w_in = even_w_in[0]
    n_main = 7 * ret_w
    wf = jnp.pad(w_in[:, n_main:], ((0, 0), (0, LANES - N_HEADS))).astype(BF16)
    z, ff = _norm_proj(h, row(norm_mix_g[0]), w_in[:, :n_main].astype(BF16), wf)
    b_f = jnp.pad(even_b_f[0].astype(F32), (0, LANES - N_HEADS)).reshape(1, LANES)
    fcum = _forget_cumsum(ff, b_f, batch, seq)
    ret = _retention(z, log_g, cos, sin_signed, row(even_ret_gn[0]), batch, seq)
    fox = _fox_attention(z, fcum, batch, seq)
    h = _proj_residual([ret, fox], even_w_out[0].astype(BF16), h)
    h = _mlp(h, row(norm_mlp_g[0]), mlp_w1[0].astype(BF16), mlp_w2[0].astype(BF16))

    lambda_init = 0.8 - 0.6 * math.exp(-0.3 * 1)
    z = _norm_proj(h, row(norm_mix_g[1]), odd_w_in[0].astype(BF16))
    att = _diff_attention(z, row(odd_lambda_q1[0]), row(odd_lambda_k1[0]),
                          row(odd_lambda_q2[0]), row(odd_lambda_k2[0]),
                          row(odd_subln_g[0]), lambda_init, batch, seq)
    h = _proj_residual([att], odd_w_out[0].astype(BF16), h)
    h = _mlp(h, row(norm_mlp_g[1]), mlp_w1[1].astype(BF16), mlp_w2[1].astype(BF16),
             row(final_g))
    return h.reshape(batch, seq, d)
```

```python
import functools
import math

import jax
import jax.numpy as jnp
from jax import lax
from jax.experimental import pallas as pl
from jax.experimental.pallas import tpu as pltpu

F32 = jnp.float32
BF16 = jnp.bfloat16

CHUNK = 64
CHUNK_SHIFT = 6
HEAD_DIM = 128
N_HEADS = 8
ROPE_BASE = 10000.0
RMS_EPS = 1e-6
GN_EPS = 1e-5
LOG2E = 1.4426950408889634
LANES = 128
SUBLANES = 8
MIB = 1024 * 1024

PROJ_TM = 1024
PROJ_TN = 1024
MLP_TM = 512
MLP_TF = 1024
RET_ROWS = 256
ATT_T = 512
CUMSUM_ROWS = 256


def _params(semantics, vmem_mib):
    return pltpu.CompilerParams(dimension_semantics=semantics,
                                vmem_limit_bytes=vmem_mib * MIB)


def _dot(a, b):
    return jnp.dot(a, b, preferred_element_type=F32)


def _dot_nt(a, b):
    return lax.dot_general(a, b, (((1,), (1,)), ((), ())), preferred_element_type=F32)


def _dot_tn(a, b):
    return lax.dot_general(a, b, (((0,), (0,)), ((), ())), preferred_element_type=F32)


def _rms_normalize(x, g):
    ms = jnp.mean(x * x, axis=-1, keepdims=True)
    return (x * lax.rsqrt(ms + RMS_EPS)) * g


def _slab_reduce(x, op):
    rows, n = x.shape
    return op(x.reshape(rows // SUBLANES, SUBLANES, n), axis=0)


def _norm_proj_kernel(x_ref, g_ref, w_ref, *rest, with_forget):
    if with_forget:
        wf_ref, o_ref, f_ref, xn_ref = rest
    else:
        o_ref, xn_ref = rest

    @pl.when(pl.program_id(1) == 0)
    def _():
        xn_ref[...] = _rms_normalize(x_ref[...], g_ref[...]).astype(BF16)
        if with_forget:
            f_ref[...] = _dot(xn_ref[...], wf_ref[...])

    o_ref[...] = _dot(xn_ref[...], w_ref[...]).astype(o_ref.dtype)


def _norm_proj(x, g, w, wf=None):
    t, d = x.shape
    n = w.shape[1]
    tm, tn = PROJ_TM, PROJ_TN
    with_forget = wf is not None
    in_specs = [
        pl.BlockSpec((tm, d), lambda i, j: (i, 0)),
        pl.BlockSpec((1, d), lambda i, j: (0, 0)),
        pl.BlockSpec((d, tn), lambda i, j: (0, j)),
    ]
    out_shape = [jax.ShapeDtypeStruct((t, n), BF16)]
    out_specs = [pl.BlockSpec((tm, tn), lambda i, j: (i, j))]
    args = [x, g, w]
    if with_forget:
        in_specs.append(pl.BlockSpec((d, LANES), lambda i, j: (0, 0)))
        out_shape.append(jax.ShapeDtypeStruct((t, LANES), F32))
        out_specs.append(pl.BlockSpec((tm, LANES), lambda i, j: (i, 0)))
        args.append(wf)
    out = pl.pallas_call(
        functools.partial(_norm_proj_kernel, with_forget=with_forget),
        grid=(t // tm, n // tn),
        in_specs=in_specs,
        out_specs=out_specs,
        out_shape=out_shape,
        scratch_shapes=[pltpu.VMEM((tm, d), BF16)],
        compiler_params=_params(("parallel", "arbitrary"), 48),
        name="norm_proj_forget" if with_forget else "norm_proj",
    )(*args)
    return out if with_forget else out[0]


def _rotate(x, cos, sin_signed):
    return x * cos + pltpu.roll(x, HEAD_DIM // 2, 1) * sin_signed


def _retention_kernel(lg_ref, q_ref, k_ref, v_ref, gate_ref, cos_ref, sin_ref, gn_ref,
                      o_ref, state_ref):
    rows = q_ref.shape[0]
    h = pl.program_id(1)

    @pl.when(pl.program_id(2) == 0)
    def _():
        state_ref[...] = jnp.zeros_like(state_ref)

    lg = lg_ref[h]
    cos = cos_ref[...]
    sin = sin_ref[...]
    q = _rotate(q_ref[...].astype(F32), cos, sin) * (HEAD_DIM ** -0.5)
    k = _rotate(k_ref[...].astype(F32), cos, sin)
    v = v_ref[...]

    row = lax.broadcasted_iota(jnp.int32, (rows, 1), 0).astype(F32)
    xi = jnp.exp(lg * (row + 1.0))
    zeta = jnp.exp(lg * (rows - 1.0 - row))
    ti = lax.broadcasted_iota(jnp.int32, (rows, rows), 0)
    si = lax.broadcasted_iota(jnp.int32, (rows, rows), 1)
    visible = (si >> CHUNK_SHIFT) <= (ti >> CHUNK_SHIFT)
    decay = jnp.where(visible, jnp.exp(lg * jnp.abs(ti - si).astype(F32)), 0.0)

    state = state_ref[...]
    scores = _dot_nt(q.astype(BF16), k.astype(BF16)) * decay
    out = _dot(scores.astype(BF16), v) + _dot((q * xi).astype(BF16), state.astype(BF16))
    block_decay = jnp.exp(jnp.full((1, HEAD_DIM), lg * rows, F32))
    state_ref[...] = state * block_decay + _dot_tn((k * zeta).astype(BF16), v)

    ms = jnp.mean(out * out, axis=-1, keepdims=True)
    y = out * lax.rsqrt(ms + GN_EPS) * gn_ref[...]
    gate = gate_ref[...].astype(F32)
    o_ref[...] = (jax.nn.silu(gate) * y).astype(o_ref.dtype)


def _retention(z, log_g, cos, sin_signed, gn, batch, seq):
    t = z.shape[0]
    rows = RET_ROWS
    nblk = seq // rows
    col = lambda off: pl.BlockSpec((rows, HEAD_DIM), lambda b, h, r: (b * nblk + r, off + h))
    tab = pl.BlockSpec((rows, HEAD_DIM), lambda b, h, r: (r, 0))
    return pl.pallas_call(
        _retention_kernel,
        grid=(batch, N_HEADS, nblk),
        in_specs=[
            pl.BlockSpec(memory_space=pltpu.SMEM),
            col(0), col(N_HEADS), col(2 * N_HEADS), col(3 * N_HEADS),
            tab, tab,
            pl.BlockSpec((1, HEAD_DIM), lambda b, h, r: (0, h)),
        ],
        out_specs=pl.BlockSpec((rows, HEAD_DIM), lambda b, h, r: (b * nblk + r, h)),
        out_shape=jax.ShapeDtypeStruct((t, N_HEADS * HEAD_DIM), BF16),
        scratch_shapes=[pltpu.VMEM((HEAD_DIM, HEAD_DIM), F32)],
        compiler_params=_params(("parallel", "parallel", "arbitrary"), 32),
        name="retention",
    )(log_g, z, z, z, z, cos, sin_signed, gn)


def _forget_bias_kernel(ff_ref, b_ref, tri_ref, o_ref):
    seq = ff_ref.shape[0]
    rows = tri_ref.shape[0]
    lane = lax.broadcasted_iota(jnp.int32, (rows, LANES), 1)
    carry = jnp.zeros((1, LANES), F32)
    for blk in range(seq // rows):
        sl = slice(blk * rows, (blk + 1) * rows)
        log_f = jax.nn.log_sigmoid(ff_ref[sl, :] + b_ref[...])
        c = jnp.dot(tri_ref[...], log_f, precision=lax.Precision.HIGHEST,
                    preferred_element_type=F32) + carry
        carry = c[rows - 1:rows, :]
        for h in range(N_HEADS):
            x = jnp.broadcast_to(c[:, h:h + 1], (rows, LANES)) * (-LOG2E)
            hi = x.astype(BF16).astype(F32)
            mid = (x - hi).astype(BF16).astype(F32)
            lo = x - hi - mid
            pieces = jnp.where(lane == 0, hi, jnp.where(lane == 1, mid,
                                                       jnp.where(lane == 2, lo, 0.0)))
            o_ref[h, sl, :] = pieces.astype(o_ref.dtype)


def _forget_bias(ff, b_f, batch, seq):
    rows = CUMSUM_ROWS
    tri = (jnp.arange(rows)[None, :] <= jnp.arange(rows)[:, None]).astype(F32)
    out = pl.pallas_call(
        _forget_bias_kernel,
        grid=(batch,),
        in_specs=[
            pl.BlockSpec((seq, LANES), lambda b: (b, 0)),
            pl.BlockSpec((1, LANES), lambda b: (0, 0)),
            pl.BlockSpec((rows, rows), lambda b: (0, 0)),
        ],
        out_specs=pl.BlockSpec((None, N_HEADS, seq, LANES), lambda b: (b, 0, 0, 0)),
        out_shape=jax.ShapeDtypeStruct((batch, N_HEADS, seq, LANES), BF16),
        compiler_params=_params(("parallel",), 32),
        name="forget_bias",
    )(ff, b_f, tri)
    return out.reshape(batch * N_HEADS * seq, LANES)


def _fox_kernel(q_ref, k_ref, v_ref, fb_ref, o_ref, vt_ref, s_ref):
    seq = q_ref.shape[0]
    t = ATT_T
    vt_ref[...] = v_ref[...].astype(F32).T.astype(BF16)
    lane = lax.broadcasted_iota(jnp.int32, (t, HEAD_DIM), 1)
    ones3 = jnp.where(lane < 3, 1.0, 0.0).astype(BF16)
    key = lax.broadcasted_iota(jnp.int32, (t, t), 0)
    qry = lax.broadcasted_iota(jnp.int32, (t, t), 1)
    causal = key <= qry
    qscale = (HEAD_DIM ** -0.5) * LOG2E

    for i in range(seq // t):
        rows_i = slice(i * t, (i + 1) * t)
        qs = (q_ref[rows_i, :].astype(F32) * qscale).astype(BF16)
        qa = jnp.concatenate([qs, ones3], axis=1)
        sbuf = s_ref.at[i % 2]
        m_run = None
        for c in range(i + 1):
            rows_c = slice(c * t, (c + 1) * t)
            ka = jnp.concatenate([k_ref[rows_c, :], fb_ref[rows_c, :]], axis=1)
            s = _dot_nt(ka, qa)
            if c == i:
                s = jnp.where(causal, s, -jnp.inf)
            sbuf[rows_c, :] = s
            tile_max = _slab_reduce(s, jnp.max)
            m_run = tile_max if m_run is None else jnp.maximum(m_run, tile_max)
        m = jnp.max(m_run, axis=0, keepdims=True)
        l_run = jnp.zeros((SUBLANES, t), F32)
        acc = jnp.zeros((HEAD_DIM, t), F32)
        for c in range(i + 1):
            rows_c = slice(c * t, (c + 1) * t)
            p = jnp.exp2(sbuf[rows_c, :] - m)
            l_run = l_run + _slab_reduce(p, jnp.sum)
            acc = acc + _dot(vt_ref[:, rows_c], p.astype(BF16))
        l = jnp.sum(l_run, axis=0, keepdims=True)
        o_ref[rows_i, :] = (acc / l).T.astype(o_ref.dtype)


def _fox_attention(z, fbias, batch, seq):
    t = z.shape[0]
    col = lambda off: pl.BlockSpec((seq, HEAD_DIM), lambda b, h: (b, off + h))
    return pl.pallas_call(
        _fox_kernel,
        grid=(batch, N_HEADS),
        in_specs=[
            col(4 * N_HEADS), col(5 * N_HEADS), col(6 * N_HEADS),
            pl.BlockSpec((seq, LANES), lambda b, h: (b * N_HEADS + h, 0)),
        ],
        out_specs=pl.BlockSpec((seq, HEAD_DIM), lambda b, h: (b, h)),
        out_shape=jax.ShapeDtypeStruct((t, N_HEADS * HEAD_DIM), BF16),
        scratch_shapes=[pltpu.VMEM((HEAD_DIM, seq), BF16),
                        pltpu.VMEM((2, seq, ATT_T), F32)],
        compiler_params=_params(("parallel", "parallel"), 40),
        name="fox_attention",
    )(z, z, z, fbias)


def _diff_kernel(q_ref, k_ref, v_ref, lq1_ref, lk1_ref, lq2_ref, lk2_ref, g_ref, o_ref,
                 vt_ref, s1_ref, s2_ref, *, lambda_init):
    seq = q_ref.shape[0]
    t = ATT_T
    vt_ref[...] = v_ref[...].astype(F32).T.astype(BF16)
    lam = (jnp.exp(jnp.sum(lq1_ref[...] * lk1_ref[...], axis=-1, keepdims=True))
           - jnp.exp(jnp.sum(lq2_ref[...] * lk2_ref[...], axis=-1, keepdims=True))
           + lambda_init)
    key = lax.broadcasted_iota(jnp.int32, (t, t), 0)
    qry = lax.broadcasted_iota(jnp.int32, (t, t), 1)
    visible = (key >> CHUNK_SHIFT) <= (qry >> CHUNK_SHIFT)
    qscale = (HEAD_DIM ** -0.5) * LOG2E
    maps = ((slice(0, HEAD_DIM), s1_ref), (slice(HEAD_DIM, 2 * HEAD_DIM), s2_ref))

    for i in range(seq // t):
        rows_i = slice(i * t, (i + 1) * t)
        qs = (q_ref[rows_i, :].astype(F32) * qscale).astype(BF16)
        inv = []
        for cols, s_ref in maps:
            m_run = None
            for c in range(i + 1):
                rows_c = slice(c * t, (c + 1) * t)
                s = _dot_nt(k_ref[rows_c, cols], qs[:, cols])
                if c == i:
                    s = jnp.where(visible, s, -jnp.inf)
                s_ref[rows_c, :] = s
                tile_max = _slab_reduce(s, jnp.max)
                m_run = tile_max if m_run is None else jnp.maximum(m_run, tile_max)
            m = jnp.max(m_run, axis=0, keepdims=True)
            l_run = jnp.zeros((SUBLANES, t), F32)
            for c in range(i + 1):
                rows_c = slice(c * t, (c + 1) * t)
                p = jnp.exp2(s_ref[rows_c, :] - m)
                s_ref[rows_c, :] = p
                l_run = l_run + _slab_reduce(p, jnp.sum)
            inv.append(1.0 / jnp.sum(l_run, axis=0, keepdims=True))
        r1 = inv[0]
        r2 = lam * inv[1]
        acc = jnp.zeros((2 * HEAD_DIM, t), F32)
        for c in range(i + 1):
            rows_c = slice(c * t, (c + 1) * t)
            a = s1_ref[rows_c, :] * r1 - s2_ref[rows_c, :] * r2
            acc = acc + _dot(vt_ref[:, rows_c], a.astype(BF16))
        o = acc.T
        ms = jnp.mean(o * o, axis=-1, keepdims=True)
        y = o * lax.rsqrt(ms + GN_EPS) * g_ref[...] * (1.0 - lambda_init)
        o_ref[rows_i, :] = y.astype(o_ref.dtype)


def _diff_attention(z, lq1, lk1, lq2, lk2, subln_g, lambda_init, batch, seq):
    t = z.shape[0]
    width = 2 * HEAD_DIM
    col = lambda off: pl.BlockSpec((seq, width), lambda b, h: (b, off + h))
    vec = pl.BlockSpec((1, HEAD_DIM), lambda b, h: (0, 0))
    return pl.pallas_call(
        functools.partial(_diff_kernel, lambda_init=lambda_init),
        grid=(batch, N_HEADS),
        in_specs=[
            col(0), col(N_HEADS), col(2 * N_HEADS),
            vec, vec, vec, vec,
            pl.BlockSpec((1, width), lambda b, h: (0, 0)),
        ],
        out_specs=pl.BlockSpec((seq, width), lambda b, h: (b, h)),
        out_shape=jax.ShapeDtypeStruct((t, N_HEADS * width), BF16),
        scratch_shapes=[pltpu.VMEM((width, seq), BF16),
                        pltpu.VMEM((seq, ATT_T), F32),
                        pltpu.VMEM((seq, ATT_T), F32)],
        compiler_params=_params(("parallel", "parallel"), 40),
        name="diff_attention",
    )(z, z, z, lq1, lk1, lq2, lk2, subln_g)


def _proj_residual_kernel(*refs, n_parts):
    a_refs = refs[:n_parts]
    w_refs = refs[n_parts:2 * n_parts]
    res_ref, o_ref = refs[2 * n_parts:]
    acc = res_ref[...]
    for a_ref, w_ref in zip(a_refs, w_refs):
        acc = acc + _dot(a_ref[...], w_ref[...])
    o_ref[...] = acc


def _proj_residual(parts, w, res):
    t, n = res.shape
    tm, tn = PROJ_TM, PROJ_TN
    kp = parts[0].shape[1]
    n_parts = len(parts)
    a_specs = [pl.BlockSpec((tm, kp), lambda i, j: (i, 0)) for _ in parts]
    w_specs = [pl.BlockSpec((kp, tn), functools.partial(lambda i, j, p: (p, j), p=p))
               for p in range(n_parts)]
    return pl.pallas_call(
        functools.partial(_proj_residual_kernel, n_parts=n_parts),
        grid=(t // tm, n // tn),
        in_specs=a_specs + w_specs + [pl.BlockSpec((tm, tn), lambda i, j: (i, j))],
        out_specs=pl.BlockSpec((tm, tn), lambda i, j: (i, j)),
        out_shape=jax.ShapeDtypeStruct((t, n), F32),
        compiler_params=_params(("parallel", "parallel"), 48),
        name="proj_residual",
    )(*parts, *([w] * n_parts), res)


def _mlp_kernel(h_ref, g_ref, w1_ref, w2_ref, *rest, with_final):
    if with_final:
        fg_ref, o_ref, xn_ref, acc_ref = rest
    else:
        o_ref, xn_ref, acc_ref = rest
    f = pl.program_id(1)

    @pl.when(f == 0)
    def _():
        xn_ref[...] = _rms_normalize(h_ref[...], g_ref[...]).astype(BF16)
        acc_ref[...] = jnp.zeros_like(acc_ref)

    a = jnp.maximum(_dot(xn_ref[...], w1_ref[...]), 0.0)
    acc_ref[...] += _dot((a * a).astype(BF16), w2_ref[...])

    @pl.when(f == pl.num_programs(1) - 1)
    def _():
        out = h_ref[...] + acc_ref[...]
        if with_final:
            out = _rms_normalize(out, fg_ref[...])
        o_ref[...] = out


def _mlp(h, g, w1, w2, final_g=None):
    t, d = h.shape
    dff = w1.shape[1]
    tm, tf = MLP_TM, MLP_TF
    with_final = final_g is not None
    in_specs = [
        pl.BlockSpec((tm, d), lambda i, f: (i, 0)),
        pl.BlockSpec((1, d), lambda i, f: (0, 0)),
        pl.BlockSpec((d, tf), lambda i, f: (0, f)),
        pl.BlockSpec((tf, d), lambda i, f: (f, 0)),
    ]
    args = [h, g, w1, w2]
    if with_final:
        in_specs.append(pl.BlockSpec((1, d), lambda i, f: (0, 0)))
        args.append(final_g)
    return pl.pallas_call(
        functools.partial(_mlp_kernel, with_final=with_final),
        grid=(t // tm, dff // tf),
        in_specs=in_specs,
        out_specs=pl.BlockSpec((tm, d), lambda i, f: (i, 0)),
        out_shape=jax.ShapeDtypeStruct((t, d), F32),
        scratch_shapes=[pltpu.VMEM((tm, d), BF16), pltpu.VMEM((tm, d), F32)],
        compiler_params=_params(("parallel", "arbitrary"), 56),
        name="mlp_final" if with_final else "mlp",
    )(*args)


def kernel(x, norm_mix_g, norm_mlp_g, even_w_in, even_b_f, even_ret_gn, even_w_out,
           odd_w_in, odd_lambda_q1, odd_lambda_k1, odd_lambda_q2, odd_lambda_k2,
           odd_subln_g, odd_w_out, mlp_w1, mlp_w2, final_g):
    batch, seq, d = x.shape
    t = batch * seq
    ret_w = N_HEADS * HEAD_DIM
    row = lambda v: v.reshape(1, -1).astype(F32)

    half = HEAD_DIM // 2
    inv = ROPE_BASE ** (-jnp.arange(half, dtype=F32) / half)
    ang = jnp.arange(seq, dtype=F32)[:, None] * inv[None, :]
    cos = jnp.concatenate([jnp.cos(ang), jnp.cos(ang)], axis=-1)
    sin_signed = jnp.concatenate([-jnp.sin(ang), jnp.sin(ang)], axis=-1)
    log_g = jnp.log1p(-(2.0 ** (-5.0 - jnp.arange(N_HEADS, dtype=F32))))

    h = x.reshape(t, d)

    w_in = even_w_in[0]
    n_main = 7 * ret_w
    wf = jnp.pad(w_in[:, n_main:], ((0, 0), (0, LANES - N_HEADS))).astype(BF16)
    z, ff = _norm_proj(h, row(norm_mix_g[0]), w_in[:, :n_main].astype(BF16), wf)
    b_f = jnp.pad(even_b_f[0].astype(F32), (0, LANES - N_HEADS)).reshape(1, LANES)
    fbias = _forget_bias(ff, b_f, batch, seq)
    ret = _retention(z, log_g, cos, sin_signed, row(even_ret_gn[0]), batch, seq)
    fox = _fox_attention(z, fbias, batch, seq)
    h = _proj_residual([ret, fox], even_w_out[0].astype(BF16), h)
    h = _mlp(h, row(norm_mlp_g[0]), mlp_w1[0].astype(BF16), mlp_w2[0].astype(BF16))

    lambda_init = 0.8 - 0.6 * math.exp(-0.3 * 1)
    z = _norm_proj(h, row(norm_mix_g[1]), odd_w_in[0].astype(BF16))
    att = _diff_attention(z, row(odd_lambda_q1[0]), row(odd_lambda_k1[0]),
                          row(odd_lambda_q2[0]), row(odd_lambda_k2[0]),
                          row(odd_subln_g[0]), lambda_init, batch, seq)
    h = _proj_residual([att], odd_w_out[0].astype(BF16), h)
    h = _mlp(h, row(norm_mlp_g[1]), mlp_w1[1].astype(BF16), mlp_w2[1].astype(BF16),
             row(final_g))
    return h.reshape(batch, seq, d)
```

```python
import functools
import math

import jax
import jax.numpy as jnp
from jax import lax
from jax.experimental import pallas as pl
from jax.experimental.pallas import tpu as pltpu

F32 = jnp.float32
BF16 = jnp.bfloat16

CHUNK = 64
CHUNK_SHIFT = 6
HEAD_DIM = 128
N_HEADS = 8
ROPE_BASE = 10000.0
RMS_EPS = 1e-6
GN_EPS = 1e-5
LOG2E = 1.4426950408889634
LANES = 128
SUBLANES = 8
MIB = 1024 * 1024

PROJ_TM = 1024
PROJ_TN = 1024
MLP_TM = 512
MLP_TF = 1024
RET_ROWS = 256
ATT_T = 512
CUMSUM_ROWS = 256


def _params(semantics, vmem_mib):
    return pltpu.CompilerParams(dimension_semantics=semantics,
                                vmem_limit_bytes=vmem_mib * MIB)


def _dot(a, b):
    return jnp.dot(a, b, preferred_element_type=F32)


def _dot_nt(a, b):
    return lax.dot_general(a, b, (((1,), (1,)), ((), ())), preferred_element_type=F32)


def _dot_tn(a, b):
    return lax.dot_general(a, b, (((0,), (0,)), ((), ())), preferred_element_type=F32)


def _rms_normalize(x, g):
    ms = jnp.mean(x * x, axis=-1, keepdims=True)
    return (x * lax.rsqrt(ms + RMS_EPS)) * g


def _slab_reduce(x, op):
    rows, n = x.shape
    return op(x.reshape(rows // SUBLANES, SUBLANES, n), axis=0)


def _norm_proj_kernel(x_ref, g_ref, w_ref, *rest, with_forget):
    if with_forget:
        wf_ref, o_ref, f_ref, xn_ref = rest
    else:
        o_ref, xn_ref = rest

    @pl.when(pl.program_id(1) == 0)
    def _():
        xn_ref[...] = _rms_normalize(x_ref[...], g_ref[...]).astype(BF16)
        if with_forget:
            f_ref[...] = _dot(xn_ref[...], wf_ref[...])

    o_ref[...] = _dot(xn_ref[...], w_ref[...].astype(BF16)).astype(o_ref.dtype)


def _norm_proj(x, g, w, n, wf=None):
    t, d = x.shape
    tm, tn = PROJ_TM, PROJ_TN
    with_forget = wf is not None
    in_specs = [
        pl.BlockSpec((tm, d), lambda i, j: (i, 0)),
        pl.BlockSpec((1, d), lambda i, j: (0, 0)),
        pl.BlockSpec((d, tn), lambda i, j: (0, j)),
    ]
    out_shape = [jax.ShapeDtypeStruct((t, n), BF16)]
    out_specs = [pl.BlockSpec((tm, tn), lambda i, j: (i, j))]
    args = [x, g, w]
    if with_forget:
        in_specs.append(pl.BlockSpec((d, LANES), lambda i, j: (0, 0)))
        out_shape.append(jax.ShapeDtypeStruct((t, LANES), F32))
        out_specs.append(pl.BlockSpec((tm, LANES), lambda i, j: (i, 0)))
        args.append(wf)
    out = pl.pallas_call(
        functools.partial(_norm_proj_kernel, with_forget=with_forget),
        grid=(t // tm, n // tn),
        in_specs=in_specs,
        out_specs=out_specs,
        out_shape=out_shape,
        scratch_shapes=[pltpu.VMEM((tm, d), BF16)],
        compiler_params=_params(("parallel", "arbitrary"), 56),
        name="norm_proj_forget" if with_forget else "norm_proj",
    )(*args)
    return out if with_forget else out[0]


def _rotate(x, cos, sin_signed):
    return x * cos + pltpu.roll(x, HEAD_DIM // 2, 1) * sin_signed


def _retention_kernel(lg_ref, q_ref, k_ref, v_ref, gate_ref, cos_ref, sin_ref, gn_ref, o_ref):
    seq = q_ref.shape[0]
    rows = RET_ROWS
    lg = lg_ref[pl.program_id(1)]

    row = lax.broadcasted_iota(jnp.int32, (rows, HEAD_DIM), 0).astype(F32)
    xi = jnp.exp(lg * (row + 1.0))
    zeta = jnp.exp(lg * (rows - 1.0 - row))
    ti = lax.broadcasted_iota(jnp.int32, (rows, rows), 0)
    si = lax.broadcasted_iota(jnp.int32, (rows, rows), 1)
    visible = (si >> CHUNK_SHIFT) <= (ti >> CHUNK_SHIFT)
    decay = jnp.where(visible, jnp.exp(lg * jnp.abs(ti - si).astype(F32)), 0.0)
    block_decay = jnp.exp(jnp.full((1, HEAD_DIM), lg * rows, F32))
    gn = gn_ref[...]

    state = jnp.zeros((HEAD_DIM, HEAD_DIM), F32)
    for r in range(seq // rows):
        sl = slice(r * rows, (r + 1) * rows)
        cos = cos_ref[sl, :]
        sin = sin_ref[sl, :]
        q = _rotate(q_ref[sl, :].astype(F32), cos, sin) * (HEAD_DIM ** -0.5)
        k = _rotate(k_ref[sl, :].astype(F32), cos, sin)
        v = v_ref[sl, :]
        scores = _dot_nt(q.astype(BF16), k.astype(BF16)) * decay
        out = _dot(scores.astype(BF16), v) + _dot((q * xi).astype(BF16), state.astype(BF16))
        state = state * block_decay + _dot_tn((k * zeta).astype(BF16), v)

        ms = jnp.mean(out * out, axis=-1, keepdims=True)
        y = out * lax.rsqrt(ms + GN_EPS) * gn
        gate = gate_ref[sl, :].astype(F32)
        o_ref[sl, :] = (jax.nn.silu(gate) * y).astype(o_ref.dtype)


def _retention(z, log_g, cos, sin_signed, gn, batch, seq):
    t = z.shape[0]
    col = lambda off: pl.BlockSpec((seq, HEAD_DIM), lambda b, h: (b, off + h))
    tab = pl.BlockSpec((seq, HEAD_DIM), lambda b, h: (0, 0))
    return pl.pallas_call(
        _retention_kernel,
        grid=(batch, N_HEADS),
        in_specs=[
            pl.BlockSpec(memory_space=pltpu.SMEM),
            col(0), col(N_HEADS), col(2 * N_HEADS), col(3 * N_HEADS),
            tab, tab,
            pl.BlockSpec((1, HEAD_DIM), lambda b, h: (0, h)),
        ],
        out_specs=pl.BlockSpec((seq, HEAD_DIM), lambda b, h: (b, h)),
        out_shape=jax.ShapeDtypeStruct((t, N_HEADS * HEAD_DIM), BF16),
        compiler_params=_params(("parallel", "parallel"), 32),
        name="retention",
    )(log_g, z, z, z, z, cos, sin_signed, gn)


def _forget_bias_kernel(ff_ref, b_ref, tri_ref, o_ref):
    seq = ff_ref.shape[0]
    rows = tri_ref.shape[0]
    lane = lax.broadcasted_iota(jnp.int32, (rows, LANES), 1)
    carry = jnp.zeros((1, LANES), F32)
    for blk in range(seq // rows):
        sl = slice(blk * rows, (blk + 1) * rows)
        log_f = jax.nn.log_sigmoid(ff_ref[sl, :] + b_ref[...])
        c = jnp.dot(tri_ref[...], log_f, precision=lax.Precision.HIGHEST,
                    preferred_element_type=F32) + carry
        carry = c[rows - 1:rows, :]
        for h in range(N_HEADS):
            x = jnp.broadcast_to(c[:, h:h + 1], (rows, LANES)) * (-LOG2E)
            hi = x.astype(BF16).astype(F32)
            mid = (x - hi).astype(BF16).astype(F32)
            lo = x - hi - mid
            pieces = jnp.where(lane == 0, hi, jnp.where(lane == 1, mid,
                                                       jnp.where(lane == 2, lo, 0.0)))
            o_ref[h, sl, :] = pieces.astype(o_ref.dtype)


def _forget_bias(ff, b_f, batch, seq):
    rows = CUMSUM_ROWS
    tri = (jnp.arange(rows)[None, :] <= jnp.arange(rows)[:, None]).astype(F32)
    out = pl.pallas_call(
        _forget_bias_kernel,
        grid=(batch,),
        in_specs=[
            pl.BlockSpec((seq, LANES), lambda b: (b, 0)),
            pl.BlockSpec((1, LANES), lambda b: (0, 0)),
            pl.BlockSpec((rows, rows), lambda b: (0, 0)),
        ],
        out_specs=pl.BlockSpec((None, N_HEADS, seq, LANES), lambda b: (b, 0, 0, 0)),
        out_shape=jax.ShapeDtypeStruct((batch, N_HEADS, seq, LANES), BF16),
        compiler_params=_params(("parallel",), 32),
        name="forget_bias",
    )(ff, b_f, tri)
    return out.reshape(batch * N_HEADS * seq, LANES)


def _fox_kernel(q_ref, k_ref, v_ref, fb_ref, o_ref, vt_ref, s_ref):
    seq = q_ref.shape[0]
    t = ATT_T
    vt_ref[...] = v_ref[...].astype(F32).T.astype(BF16)
    lane = lax.broadcasted_iota(jnp.int32, (t, HEAD_DIM), 1)
    ones3 = jnp.where(lane < 3, 1.0, 0.0).astype(BF16)
    key = lax.broadcasted_iota(jnp.int32, (t, t), 0)
    qry = lax.broadcasted_iota(jnp.int32, (t, t), 1)
    causal = key <= qry
    qscale = (HEAD_DIM ** -0.5) * LOG2E

    for i in range(seq // t):
        rows_i = slice(i * t, (i + 1) * t)
        qs = (q_ref[rows_i, :].astype(F32) * qscale).astype(BF16)
        qa = jnp.concatenate([qs, ones3], axis=1)
        sbuf = s_ref.at[i % 2]
        m_run = None
        for c in range(i + 1):
            rows_c = slice(c * t, (c + 1) * t)
            ka = jnp.concatenate([k_ref[rows_c, :], fb_ref[rows_c, :]], axis=1)
            s = _dot_nt(ka, qa)
            if c == i:
                s = jnp.where(causal, s, -jnp.inf)
            sbuf[rows_c, :] = s
            tile_max = _slab_reduce(s, jnp.max)
            m_run = tile_max if m_run is None else jnp.maximum(m_run, tile_max)
        m = jnp.max(m_run, axis=0, keepdims=True)
        l_run = jnp.zeros((SUBLANES, t), F32)
        acc = jnp.zeros((HEAD_DIM, t), F32)
        for c in range(i + 1):
            rows_c = slice(c * t, (c + 1) * t)
            p = jnp.exp2(sbuf[rows_c, :] - m)
            l_run = l_run + _slab_reduce(p, jnp.sum)
            acc = acc + _dot(vt_ref[:, rows_c], p.astype(BF16))
        l = jnp.sum(l_run, axis=0, keepdims=True)
        o_ref[rows_i, :] = (acc / l).T.astype(o_ref.dtype)


def _fox_attention(z, fbias, batch, seq):
    t = z.shape[0]
    col = lambda off: pl.BlockSpec((seq, HEAD_DIM), lambda b, h: (b, off + h))
    return pl.pallas_call(
        _fox_kernel,
        grid=(batch, N_HEADS),
        in_specs=[
            col(4 * N_HEADS), col(5 * N_HEADS), col(6 * N_HEADS),
            pl.BlockSpec((seq, LANES), lambda b, h: (b * N_HEADS + h, 0)),
        ],
        out_specs=pl.BlockSpec((seq, HEAD_DIM), lambda b, h: (b, h)),
        out_shape=jax.ShapeDtypeStruct((t, N_HEADS * HEAD_DIM), BF16),
        scratch_shapes=[pltpu.VMEM((HEAD_DIM, seq), BF16),
                        pltpu.VMEM((2, seq, ATT_T), F32)],
        compiler_params=_params(("parallel", "parallel"), 40),
        name="fox_attention",
    )(z, z, z, fbias)


def _diff_kernel(q_ref, k_ref, v_ref, lq1_ref, lk1_ref, lq2_ref, lk2_ref, g_ref, o_ref,
                 vt_ref, s1_ref, s2_ref, *, lambda_init):
    seq = q_ref.shape[0]
    t = ATT_T
    vt_ref[...] = v_ref[...].astype(F32).T.astype(BF16)
    lam = (jnp.exp(jnp.sum(lq1_ref[...] * lk1_ref[...], axis=-1, keepdims=True))
           - jnp.exp(jnp.sum(lq2_ref[...] * lk2_ref[...], axis=-1, keepdims=True))
           + lambda_init)
    key = lax.broadcasted_iota(jnp.int32, (t, t), 0)
    qry = lax.broadcasted_iota(jnp.int32, (t, t), 1)
    visible = (key >> CHUNK_SHIFT) <= (qry >> CHUNK_SHIFT)
    qscale = (HEAD_DIM ** -0.5) * LOG2E
    maps = ((slice(0, HEAD_DIM), s1_ref), (slice(HEAD_DIM, 2 * HEAD_DIM), s2_ref))

    for i in range(seq // t):
        rows_i = slice(i * t, (i + 1) * t)
        qs = (q_ref[rows_i, :].astype(F32) * qscale).astype(BF16)
        inv = []
        for cols, s_ref in maps:
            m_run = None
            for c in range(i + 1):
                rows_c = slice(c * t, (c + 1) * t)
                s = _dot_nt(k_ref[rows_c, cols], qs[:, cols])
                if c == i:
                    s = jnp.where(visible, s, -jnp.inf)
                s_ref[rows_c, :] = s
                tile_max = _slab_reduce(s, jnp.max)
                m_run = tile_max if m_run is None else jnp.maximum(m_run, tile_max)
            m = jnp.max(m_run, axis=0, keepdims=True)
            l_run = jnp.zeros((SUBLANES, t), F32)
            for c in range(i + 1):
                rows_c = slice(c * t, (c + 1) * t)
                p = jnp.exp2(s_ref[rows_c, :] - m)
                s_ref[rows_c, :] = p
                l_run = l_run + _slab_reduce(p, jnp.sum)
            inv.append(1.0 / jnp.sum(l_run, axis=0, keepdims=True))
        r1 = inv[0]
        r2 = lam * inv[1]
        acc = jnp.zeros((2 * HEAD_DIM, t), F32)
        for c in range(i + 1):
            rows_c = slice(c * t, (c + 1) * t)
            a = s1_ref[rows_c, :] * r1 - s2_ref[rows_c, :] * r2
            acc = acc + _dot(vt_ref[:, rows_c], a.astype(BF16))
        o = acc.T
        ms = jnp.mean(o * o, axis=-1, keepdims=True)
        y = o * lax.rsqrt(ms + GN_EPS) * g_ref[...] * (1.0 - lambda_init)
        o_ref[rows_i, :] = y.astype(o_ref.dtype)


def _diff_attention(z, lq1, lk1, lq2, lk2, subln_g, lambda_init, batch, seq):
    t = z.shape[0]
    width = 2 * HEAD_DIM
    col = lambda off: pl.BlockSpec((seq, width), lambda b, h: (b, off + h))
    vec = pl.BlockSpec((1, HEAD_DIM), lambda b, h: (0, 0))
    return pl.pallas_call(
        functools.partial(_diff_kernel, lambda_init=lambda_init),
        grid=(batch, N_HEADS),
        in_specs=[
            col(0), col(N_HEADS), col(2 * N_HEADS),
            vec, vec, vec, vec,
            pl.BlockSpec((1, width), lambda b, h: (0, 0)),
        ],
        out_specs=pl.BlockSpec((seq, width), lambda b, h: (b, h)),
        out_shape=jax.ShapeDtypeStruct((t, N_HEADS * width), BF16),
        scratch_shapes=[pltpu.VMEM((width, seq), BF16),
                        pltpu.VMEM((seq, ATT_T), F32),
                        pltpu.VMEM((seq, ATT_T), F32)],
        compiler_params=_params(("parallel", "parallel"), 40),
        name="diff_attention",
    )(z, z, z, lq1, lk1, lq2, lk2, subln_g)


def _proj_residual_kernel(*refs, n_parts):
    a_refs = refs[:n_parts]
    w_refs = refs[n_parts:2 * n_parts]
    res_ref, o_ref = refs[2 * n_parts:]
    acc = res_ref[...]
    for a_ref, w_ref in zip(a_refs, w_refs):
        acc = acc + _dot(a_ref[...], w_ref[...])
    o_ref[...] = acc


def _proj_residual(parts, w, res):
    t, n = res.shape
    tm, tn = PROJ_TM, PROJ_TN
    kp = parts[0].shape[1]
    n_parts = len(parts)
    a_specs = [pl.BlockSpec((tm, kp), lambda i, j: (i, 0)) for _ in parts]
    w_specs = [pl.BlockSpec((kp, tn), functools.partial(lambda i, j, p: (p, j), p=p))
               for p in range(n_parts)]
    return pl.pallas_call(
        functools.partial(_proj_residual_kernel, n_parts=n_parts),
        grid=(t // tm, n // tn),
        in_specs=a_specs + w_specs + [pl.BlockSpec((tm, tn), lambda i, j: (i, j))],
        out_specs=pl.BlockSpec((tm, tn), lambda i, j: (i, j)),
        out_shape=jax.ShapeDtypeStruct((t, n), F32),
        compiler_params=_params(("parallel", "parallel"), 48),
        name="proj_residual",
    )(*parts, *([w] * n_parts), res)


def _mlp_kernel(h_ref, g_ref, w1_ref, w2_ref, *rest, with_final):
    if with_final:
        fg_ref, o_ref, xn_ref, acc_ref = rest
    else:
        o_ref, xn_ref, acc_ref = rest
    f = pl.program_id(1)

    @pl.when(f == 0)
    def _():
        xn_ref[...] = _rms_normalize(h_ref[...], g_ref[...]).astype(BF16)
        acc_ref[...] = jnp.zeros_like(acc_ref)

    a = jnp.maximum(_dot(xn_ref[...], w1_ref[...]), 0.0)
    acc_ref[...] += _dot((a * a).astype(BF16), w2_ref[...])

    @pl.when(f == pl.num_programs(1) - 1)
    def _():
        out = h_ref[...] + acc_ref[...]
        if with_final:
            out = _rms_normalize(out, fg_ref[...])
        o_ref[...] = out


def _mlp(h, g, w1, w2, final_g=None):
    t, d = h.shape
    dff = w1.shape[1]
    tm, tf = MLP_TM, MLP_TF
    with_final = final_g is not None
    in_specs = [
        pl.BlockSpec((tm, d), lambda i, f: (i, 0)),
        pl.BlockSpec((1, d), lambda i, f: (0, 0)),
        pl.BlockSpec((d, tf), lambda i, f: (0, f)),
        pl.BlockSpec((tf, d), lambda i, f: (f, 0)),
    ]
    args = [h, g, w1, w2]
    if with_final:
        in_specs.append(pl.BlockSpec((1, d), lambda i, f: (0, 0)))
        args.append(final_g)
    return pl.pallas_call(
        functools.partial(_mlp_kernel, with_final=with_final),
        grid=(t // tm, dff // tf),
        in_specs=in_specs,
        out_specs=pl.BlockSpec((tm, d), lambda i, f: (i, 0)),
        out_shape=jax.ShapeDtypeStruct((t, d), F32),
        scratch_shapes=[pltpu.VMEM((tm, d), BF16), pltpu.VMEM((tm, d), F32)],
        compiler_params=_params(("parallel", "arbitrary"), 56),
        name="mlp_final" if with_final else "mlp",
    )(*args)


def kernel(x, norm_mix_g, norm_mlp_g, even_w_in, even_b_f, even_ret_gn, even_w_out,
           odd_w_in, odd_lambda_q1, odd_lambda_k1, odd_lambda_q2, odd_lambda_k2,
           odd_subln_g, odd_w_out, mlp_w1, mlp_w2, final_g):
    batch, seq, d = x.shape
    t = batch * seq
    ret_w = N_HEADS * HEAD_DIM
    row = lambda v: v.reshape(1, -1).astype(F32)

    half = HEAD_DIM // 2
    inv = ROPE_BASE ** (-jnp.arange(half, dtype=F32) / half)
    ang = jnp.arange(seq, dtype=F32)[:, None] * inv[None, :]
    cos = jnp.concatenate([jnp.cos(ang), jnp.cos(ang)], axis=-1)
    sin_signed = jnp.concatenate([-jnp.sin(ang), jnp.sin(ang)], axis=-1)
    log_g = jnp.log1p(-(2.0 ** (-5.0 - jnp.arange(N_HEADS, dtype=F32))))

    h = x.reshape(t, d)

    w_in = even_w_in[0]
    n_main = 7 * ret_w
    wf = jnp.pad(w_in[:, n_main:], ((0, 0), (0, LANES - N_HEADS))).astype(BF16)
    z, ff = _norm_proj(h, row(norm_mix_g[0]), w_in, n_main, wf)
    b_f = jnp.pad(even_b_f[0].astype(F32), (0, LANES - N_HEADS)).reshape(1, LANES)
    fbias = _forget_bias(ff, b_f, batch, seq)
    ret = _retention(z, log_g, cos, sin_signed, row(even_ret_gn[0]), batch, seq)
    fox = _fox_attention(z, fbias, batch, seq)
    h = _proj_residual([ret, fox], even_w_out[0].astype(BF16), h)
    h = _mlp(h, row(norm_mlp_g[0]), mlp_w1[0].astype(BF16), mlp_w2[0].astype(BF16))

    lambda_init = 0.8 - 0.6 * math.exp(-0.3 * 1)
    z = _norm_proj(h, row(norm_mix_g[1]), odd_w_in[0], odd_w_in.shape[2])
    att = _diff_attention(z, row(odd_lambda_q1[0]), row(odd_lambda_k1[0]),
                          row(odd_lambda_q2[0]), row(odd_lambda_k2[0]),
                          row(odd_subln_g[0]), lambda_init, batch, seq)
    h = _proj_residual([att], odd_w_out[0].astype(BF16), h)
    h = _mlp(h, row(norm_mlp_g[1]), mlp_w1[1].astype(BF16), mlp_w2[1].astype(BF16),
             row(final_g))
    return h.reshape(batch, seq, d)
```

```python
import functools
import math

import jax
import jax.numpy as jnp
from jax import lax
from jax.experimental import pallas as pl
from jax.experimental.pallas import tpu as pltpu

F32 = jnp.float32
BF16 = jnp.bfloat16

CHUNK = 64
CHUNK_SHIFT = 6
HEAD_DIM = 128
N_HEADS = 8
ROPE_BASE = 10000.0
RMS_EPS = 1e-6
GN_EPS = 1e-5
LOG2E = 1.4426950408889634
LANES = 128
SUBLANES = 8
MIB = 1024 * 1024

PROJ_TM = 1024
PROJ_TN = 1024
MLP_TM = 512
MLP_TF = 1024
RET_ROWS = 256
ATT_T = 512
CUMSUM_ROWS = 256


def _params(semantics, vmem_mib):
    return pltpu.CompilerParams(dimension_semantics=semantics,
                                vmem_limit_bytes=vmem_mib * MIB)


def _dot(a, b):
    return jnp.dot(a, b, preferred_element_type=F32)


def _dot_nt(a, b):
    return lax.dot_general(a, b, (((1,), (1,)), ((), ())), preferred_element_type=F32)


def _dot_tn(a, b):
    return lax.dot_general(a, b, (((0,), (0,)), ((), ())), preferred_element_type=F32)


def _rms_normalize(x, g):
    ms = jnp.mean(x * x, axis=-1, keepdims=True)
    return (x * lax.rsqrt(ms + RMS_EPS)) * g


def _slab_reduce(x, op):
    rows, n = x.shape
    return op(x.reshape(rows // SUBLANES, SUBLANES, n), axis=0)


def _norm_proj_kernel(x_ref, g_ref, w_ref, *rest, with_forget):
    if with_forget:
        wf_ref, o_ref, f_ref, xn_ref = rest
    else:
        o_ref, xn_ref = rest

    @pl.when(pl.program_id(1) == 0)
    def _():
        xn_ref[...] = _rms_normalize(x_ref[...], g_ref[...]).astype(BF16)
        if with_forget:
            f_ref[...] = _dot(xn_ref[...], wf_ref[...])

    o_ref[...] = _dot(xn_ref[...], w_ref[...]).astype(o_ref.dtype)


def _norm_proj(x, g, w, layer, n, wf=None):
    t, d = x.shape
    tm, tn = PROJ_TM, PROJ_TN
    with_forget = wf is not None
    in_specs = [
        pl.BlockSpec((tm, d), lambda i, j: (i, 0)),
        pl.BlockSpec((1, d), lambda i, j: (0, 0)),
        pl.BlockSpec((None, d, tn), lambda i, j: (layer, 0, j)),
    ]
    out_shape = [jax.ShapeDtypeStruct((t, n), BF16)]
    out_specs = [pl.BlockSpec((tm, tn), lambda i, j: (i, j))]
    args = [x, g, w]
    if with_forget:
        in_specs.append(pl.BlockSpec((d, LANES), lambda i, j: (0, 0)))
        out_shape.append(jax.ShapeDtypeStruct((t, LANES), F32))
        out_specs.append(pl.BlockSpec((tm, LANES), lambda i, j: (i, 0)))
        args.append(wf)
    out = pl.pallas_call(
        functools.partial(_norm_proj_kernel, with_forget=with_forget),
        grid=(t // tm, n // tn),
        in_specs=in_specs,
        out_specs=out_specs,
        out_shape=out_shape,
        scratch_shapes=[pltpu.VMEM((tm, d), BF16)],
        compiler_params=_params(("parallel", "arbitrary"), 56),
        name="norm_proj_forget" if with_forget else "norm_proj",
    )(*args)
    return out if with_forget else out[0]


def _rotate(x, cos, sin_signed):
    return x * cos + pltpu.roll(x, HEAD_DIM // 2, 1) * sin_signed


def _retention_kernel(lg_ref, q_ref, k_ref, v_ref, gate_ref, cos_ref, sin_ref, gn_ref, o_ref):
    seq = q_ref.shape[0]
    rows = RET_ROWS
    lg = lg_ref[pl.program_id(1)]

    row = lax.broadcasted_iota(jnp.int32, (rows, HEAD_DIM), 0).astype(F32)
    xi = jnp.exp(lg * (row + 1.0))
    zeta = jnp.exp(lg * (rows - 1.0 - row))
    ti = lax.broadcasted_iota(jnp.int32, (rows, rows), 0)
    si = lax.broadcasted_iota(jnp.int32, (rows, rows), 1)
    visible = (si >> CHUNK_SHIFT) <= (ti >> CHUNK_SHIFT)
    decay = jnp.where(visible, jnp.exp(lg * jnp.abs(ti - si).astype(F32)), 0.0)
    block_decay = jnp.exp(jnp.full((1, HEAD_DIM), lg * rows, F32))
    gn = gn_ref[...]

    state = jnp.zeros((HEAD_DIM, HEAD_DIM), F32)
    for r in range(seq // rows):
        sl = slice(r * rows, (r + 1) * rows)
        cos = cos_ref[sl, :]
        sin = sin_ref[sl, :]
        q = _rotate(q_ref[sl, :].astype(F32), cos, sin) * (HEAD_DIM ** -0.5)
        k = _rotate(k_ref[sl, :].astype(F32), cos, sin)
        v = v_ref[sl, :]
        scores = _dot_nt(q.astype(BF16), k.astype(BF16)) * decay
        out = _dot(scores.astype(BF16), v) + _dot((q * xi).astype(BF16), state.astype(BF16))
        state = state * block_decay + _dot_tn((k * zeta).astype(BF16), v)

        ms = jnp.mean(out * out, axis=-1, keepdims=True)
        y = out * lax.rsqrt(ms + GN_EPS) * gn
        gate = gate_ref[sl, :].astype(F32)
        o_ref[sl, :] = (jax.nn.silu(gate) * y).astype(o_ref.dtype)


def _retention(z, log_g, cos, sin_signed, gn, batch, seq):
    t = z.shape[0]
    col = lambda off: pl.BlockSpec((seq, HEAD_DIM), lambda b, h: (b, off + h))
    tab = pl.BlockSpec((seq, HEAD_DIM), lambda b, h: (0, 0))
    return pl.pallas_call(
        _retention_kernel,
        grid=(batch, N_HEADS),
        in_specs=[
            pl.BlockSpec(memory_space=pltpu.SMEM),
            col(0), col(N_HEADS), col(2 * N_HEADS), col(3 * N_HEADS),
            tab, tab,
            pl.BlockSpec((1, HEAD_DIM), lambda b, h: (0, h)),
        ],
        out_specs=pl.BlockSpec((seq, HEAD_DIM), lambda b, h: (b, h)),
        out_shape=jax.ShapeDtypeStruct((t, N_HEADS * HEAD_DIM), BF16),
        compiler_params=_params(("parallel", "parallel"), 32),
        name="retention",
    )(log_g, z, z, z, z, cos, sin_signed, gn)


def _forget_bias_kernel(ff_ref, b_ref, tri_ref, o_ref):
    seq = ff_ref.shape[0]
    rows = tri_ref.shape[0]
    lane = lax.broadcasted_iota(jnp.int32, (rows, LANES), 1)
    carry = jnp.zeros((1, LANES), F32)
    for blk in range(seq // rows):
        sl = slice(blk * rows, (blk + 1) * rows)
        log_f = jax.nn.log_sigmoid(ff_ref[sl, :] + b_ref[...])
        c = jnp.dot(tri_ref[...], log_f, precision=lax.Precision.HIGHEST,
                    preferred_element_type=F32) + carry
        carry = c[rows - 1:rows, :]
        for h in range(N_HEADS):
            x = jnp.broadcast_to(c[:, h:h + 1], (rows, LANES)) * (-LOG2E)
            hi = x.astype(BF16).astype(F32)
            mid = (x - hi).astype(BF16).astype(F32)
            lo = x - hi - mid
            pieces = jnp.where(lane == 0, hi, jnp.where(lane == 1, mid,
                                                       jnp.where(lane == 2, lo, 0.0)))
            o_ref[h, sl, :] = pieces.astype(o_ref.dtype)


def _forget_bias(ff, b_f, batch, seq):
    rows = CUMSUM_ROWS
    tri = (jnp.arange(rows)[None, :] <= jnp.arange(rows)[:, None]).astype(F32)
    out = pl.pallas_call(
        _forget_bias_kernel,
        grid=(batch,),
        in_specs=[
            pl.BlockSpec((seq, LANES), lambda b: (b, 0)),
            pl.BlockSpec((1, LANES), lambda b: (0, 0)),
            pl.BlockSpec((rows, rows), lambda b: (0, 0)),
        ],
        out_specs=pl.BlockSpec((None, N_HEADS, seq, LANES), lambda b: (b, 0, 0, 0)),
        out_shape=jax.ShapeDtypeStruct((batch, N_HEADS, seq, LANES), BF16),
        compiler_params=_params(("parallel",), 32),
        name="forget_bias",
    )(ff, b_f, tri)
    return out.reshape(batch * N_HEADS * seq, LANES)


def _pipeline_blocks(blocks):
    for task in blocks[0][0]:
        task()
    for n, (_, value_tasks) in enumerate(blocks):
        score_tasks = blocks[n + 1][0] if n + 1 < len(blocks) else []
        done = 0
        for idx, task in enumerate(value_tasks):
            task()
            while done < len(score_tasks) and done * len(value_tasks) < (idx + 1) * len(score_tasks):
                score_tasks[done]()
                done += 1


def _fox_kernel(q_ref, k_ref, v_ref, fb_ref, o_ref, vt_ref, s_ref):
    seq = q_ref.shape[0]
    t = ATT_T
    vt_ref[...] = v_ref[...].astype(F32).T.astype(BF16)
    lane = lax.broadcasted_iota(jnp.int32, (t, HEAD_DIM), 1)
    ones3 = jnp.where(lane < 3, 1.0, 0.0).astype(BF16)
    key = lax.broadcasted_iota(jnp.int32, (t, t), 0)
    qry = lax.broadcasted_iota(jnp.int32, (t, t), 1)
    causal = key <= qry
    qscale = (HEAD_DIM ** -0.5) * LOG2E

    def query_block(i):
        rows_i = slice(i * t, (i + 1) * t)
        sbuf = s_ref.at[i % 2]
        st = {"m_run": None, "l_run": jnp.zeros((SUBLANES, t), F32),
              "acc": jnp.zeros((HEAD_DIM, t), F32)}

        def score_task(c):
            rows_c = slice(c * t, (c + 1) * t)
            if c == 0:
                qs = (q_ref[rows_i, :].astype(F32) * qscale).astype(BF16)
                st["qa"] = jnp.concatenate([qs, ones3], axis=1)
            ka = jnp.concatenate([k_ref[rows_c, :], fb_ref[rows_c, :]], axis=1)
            s = _dot_nt(ka, st["qa"])
            if c == i:
                s = jnp.where(causal, s, -jnp.inf)
            sbuf[rows_c, :] = s
            tile_max = _slab_reduce(s, jnp.max)
            st["m_run"] = tile_max if c == 0 else jnp.maximum(st["m_run"], tile_max)

        def value_task(c):
            rows_c = slice(c * t, (c + 1) * t)
            if c == 0:
                st["m"] = jnp.max(st["m_run"], axis=0, keepdims=True)
            p = jnp.exp2(sbuf[rows_c, :] - st["m"])
            st["l_run"] = st["l_run"] + _slab_reduce(p, jnp.sum)
            st["acc"] = st["acc"] + _dot(vt_ref[:, rows_c], p.astype(BF16))
            if c == i:
                l = jnp.sum(st["l_run"], axis=0, keepdims=True)
                o_ref[rows_i, :] = (st["acc"] / l).T.astype(o_ref.dtype)

        tiles = range(i + 1)
        return ([functools.partial(score_task, c) for c in tiles],
                [functools.partial(value_task, c) for c in tiles])

    _pipeline_blocks([query_block(i) for i in range(seq // t)])


def _fox_attention(z, fbias, batch, seq):
    t = z.shape[0]
    col = lambda off: pl.BlockSpec((seq, HEAD_DIM), lambda b, h: (b, off + h))
    return pl.pallas_call(
        _fox_kernel,
        grid=(batch, N_HEADS),
        in_specs=[
            col(4 * N_HEADS), col(5 * N_HEADS), col(6 * N_HEADS),
            pl.BlockSpec((seq, LANES), lambda b, h: (b * N_HEADS + h, 0)),
        ],
        out_specs=pl.BlockSpec((seq, HEAD_DIM), lambda b, h: (b, h)),
        out_shape=jax.ShapeDtypeStruct((t, N_HEADS * HEAD_DIM), BF16),
        scratch_shapes=[pltpu.VMEM((HEAD_DIM, seq), BF16),
                        pltpu.VMEM((2, seq, ATT_T), F32)],
        compiler_params=_params(("parallel", "parallel"), 40),
        name="fox_attention",
    )(z, z, z, fbias)


def _diff_kernel(q_ref, k_ref, v_ref, lq1_ref, lk1_ref, lq2_ref, lk2_ref, g_ref, o_ref,
                 vt_ref, s1_ref, s2_ref, p1_ref, p2_ref, acc_ref, *, lambda_init):
    seq = q_ref.shape[0]
    t = ATT_T
    vt_ref[...] = v_ref[...].astype(F32).T.astype(BF16)
    lam = (jnp.exp(jnp.sum(lq1_ref[...] * lk1_ref[...], axis=-1, keepdims=True))
           - jnp.exp(jnp.sum(lq2_ref[...] * lk2_ref[...], axis=-1, keepdims=True))
           + lambda_init)
    key = lax.broadcasted_iota(jnp.int32, (t, t), 0)
    qry = lax.broadcasted_iota(jnp.int32, (t, t), 1)
    visible = (key >> CHUNK_SHIFT) <= (qry >> CHUNK_SHIFT)
    qscale = (HEAD_DIM ** -0.5) * LOG2E
    col_slices = (slice(0, HEAD_DIM), slice(HEAD_DIM, 2 * HEAD_DIM))
    p_refs = (p1_ref, p2_ref)

    def query_block(i):
        rows_i = slice(i * t, (i + 1) * t)
        s_refs = (s1_ref.at[i % 2], s2_ref.at[i % 2])
        zero = jnp.zeros((SUBLANES, t), F32)
        st = {"m_run": [None, None], "m": [None, None], "l_run": [zero, zero]}

        def score_task(mi, c):
            rows_c = slice(c * t, (c + 1) * t)
            if mi == 0 and c == 0:
                st["qs"] = (q_ref[rows_i, :].astype(F32) * qscale).astype(BF16)
            cols = col_slices[mi]
            s = _dot_nt(k_ref[rows_c, cols], st["qs"][:, cols])
            if c == i:
                s = jnp.where(visible, s, -jnp.inf)
            s_refs[mi][rows_c, :] = s
            tile_max = _slab_reduce(s, jnp.max)
            st["m_run"][mi] = tile_max if c == 0 else jnp.maximum(st["m_run"][mi], tile_max)

        def exp_task(mi, c):
            rows_c = slice(c * t, (c + 1) * t)
            if c == 0:
                st["m"][mi] = jnp.max(st["m_run"][mi], axis=0, keepdims=True)
            p = jnp.exp2(s_refs[mi][rows_c, :] - st["m"][mi])
            p_refs[mi][rows_c, :] = p.astype(BF16)
            st["l_run"][mi] = st["l_run"][mi] + _slab_reduce(p, jnp.sum)

        def value_task(c):
            rows_c = slice(c * t, (c + 1) * t)
            if c == 0:
                st["l1"] = jnp.sum(st["l_run"][0], axis=0, keepdims=True)
                l2 = jnp.sum(st["l_run"][1], axis=0, keepdims=True)
                st["rho"] = (lam * st["l1"] / l2).astype(BF16)
            a = p1_ref[rows_c, :] - p2_ref[rows_c, :] * st["rho"]
            av = _dot(vt_ref[:, rows_c], a)
            acc_ref[...] = av if c == 0 else acc_ref[...] + av
            if c == i:
                o = (acc_ref[...] / st["l1"]).T
                ms = jnp.mean(o * o, axis=-1, keepdims=True)
                y = o * lax.rsqrt(ms + GN_EPS) * g_ref[...] * (1.0 - lambda_init)
                o_ref[rows_i, :] = y.astype(o_ref.dtype)

        tiles = range(i + 1)
        scores = [functools.partial(score_task, mi, c) for mi in (0, 1) for c in tiles]
        values = ([functools.partial(exp_task, mi, c) for mi in (0, 1) for c in tiles]
                  + [functools.partial(value_task, c) for c in tiles])
        return scores, values

    _pipeline_blocks([query_block(i) for i in range(seq // t)])


def _diff_attention(z, lq1, lk1, lq2, lk2, subln_g, lambda_init, batch, seq):
    t = z.shape[0]
    width = 2 * HEAD_DIM
    col = lambda off: pl.BlockSpec((seq, width), lambda b, h: (b, off + h))
    vec = pl.BlockSpec((1, HEAD_DIM), lambda b, h: (0, 0))
    return pl.pallas_call(
        functools.partial(_diff_kernel, lambda_init=lambda_init),
        grid=(batch, N_HEADS),
        in_specs=[
            col(0), col(N_HEADS), col(2 * N_HEADS),
            vec, vec, vec, vec,
            pl.BlockSpec((1, width), lambda b, h: (0, 0)),
        ],
        out_specs=pl.BlockSpec((seq, width), lambda b, h: (b, h)),
        out_shape=jax.ShapeDtypeStruct((t, N_HEADS * width), BF16),
        scratch_shapes=[pltpu.VMEM((width, seq), BF16),
                        pltpu.VMEM((2, seq, ATT_T), F32),
                        pltpu.VMEM((2, seq, ATT_T), F32),
                        pltpu.VMEM((seq, ATT_T), BF16),
                        pltpu.VMEM((seq, ATT_T), BF16),
                        pltpu.VMEM((width, ATT_T), F32)],
        compiler_params=_params(("parallel", "parallel"), 48),
        name="diff_attention",
    )(z, z, z, lq1, lk1, lq2, lk2, subln_g)


def _proj_residual_kernel(*refs, n_parts):
    a_refs = refs[:n_parts]
    w_refs = refs[n_parts:2 * n_parts]
    res_ref, o_ref = refs[2 * n_parts:]
    acc = res_ref[...]
    for a_ref, w_ref in zip(a_refs, w_refs):
        acc = acc + _dot(a_ref[...], w_ref[...])
    o_ref[...] = acc


def _proj_residual(parts, w, layer, res):
    t, n = res.shape
    tm, tn = PROJ_TM, PROJ_TN
    kp = parts[0].shape[1]
    n_parts = len(parts)
    a_specs = [pl.BlockSpec((tm, kp), lambda i, j: (i, 0)) for _ in parts]
    w_specs = [pl.BlockSpec((None, kp, tn),
                            functools.partial(lambda i, j, p: (layer, p, j), p=p))
               for p in range(n_parts)]
    return pl.pallas_call(
        functools.partial(_proj_residual_kernel, n_parts=n_parts),
        grid=(t // tm, n // tn),
        in_specs=a_specs + w_specs + [pl.BlockSpec((tm, tn), lambda i, j: (i, j))],
        out_specs=pl.BlockSpec((tm, tn), lambda i, j: (i, j)),
        out_shape=jax.ShapeDtypeStruct((t, n), F32),
        compiler_params=_params(("parallel", "parallel"), 48),
        name="proj_residual",
    )(*parts, *([w] * n_parts), res)


def _mlp_kernel(h_ref, g_ref, w1_ref, w2_ref, *rest, with_final):
    if with_final:
        fg_ref, o_ref, xn_ref, acc_ref = rest
    else:
        o_ref, xn_ref, acc_ref = rest
    f = pl.program_id(1)

    @pl.when(f == 0)
    def _():
        xn_ref[...] = _rms_normalize(h_ref[...], g_ref[...]).astype(BF16)
        acc_ref[...] = jnp.zeros_like(acc_ref)

    a = jnp.maximum(_dot(xn_ref[...], w1_ref[...]), 0.0)
    acc_ref[...] += _dot((a * a).astype(BF16), w2_ref[...])

    @pl.when(f == pl.num_programs(1) - 1)
    def _():
        out = h_ref[...] + acc_ref[...]
        if with_final:
            out = _rms_normalize(out, fg_ref[...])
        o_ref[...] = out


def _mlp(h, g, w1, w2, layer, final_g=None):
    t, d = h.shape
    dff = w1.shape[2]
    tm, tf = MLP_TM, MLP_TF
    with_final = final_g is not None
    in_specs = [
        pl.BlockSpec((tm, d), lambda i, f: (i, 0)),
        pl.BlockSpec((1, d), lambda i, f: (0, 0)),
        pl.BlockSpec((None, d, tf), lambda i, f: (layer, 0, f)),
        pl.BlockSpec((None, tf, d), lambda i, f: (layer, f, 0)),
    ]
    args = [h, g, w1, w2]
    if with_final:
        in_specs.append(pl.BlockSpec((1, d), lambda i, f: (0, 0)))
        args.append(final_g)
    return pl.pallas_call(
        functools.partial(_mlp_kernel, with_final=with_final),
        grid=(t // tm, dff // tf),
        in_specs=in_specs,
        out_specs=pl.BlockSpec((tm, d), lambda i, f: (i, 0)),
        out_shape=jax.ShapeDtypeStruct((t, d), F32),
        scratch_shapes=[pltpu.VMEM((tm, d), BF16), pltpu.VMEM((tm, d), F32)],
        compiler_params=_params(("parallel", "arbitrary"), 56),
        name="mlp_final" if with_final else "mlp",
    )(*args)


def kernel(x, norm_mix_g, norm_mlp_g, even_w_in, even_b_f, even_ret_gn, even_w_out,
           odd_w_in, odd_lambda_q1, odd_lambda_k1, odd_lambda_q2, odd_lambda_k2,
           odd_subln_g, odd_w_out, mlp_w1, mlp_w2, final_g):
    batch, seq, d = x.shape
    t = batch * seq
    ret_w = N_HEADS * HEAD_DIM
    row = lambda v: v.reshape(1, -1).astype(F32)

    half = HEAD_DIM // 2
    inv = ROPE_BASE ** (-jnp.arange(half, dtype=F32) / half)
    ang = jnp.arange(seq, dtype=F32)[:, None] * inv[None, :]
    cos = jnp.concatenate([jnp.cos(ang), jnp.cos(ang)], axis=-1)
    sin_signed = jnp.concatenate([-jnp.sin(ang), jnp.sin(ang)], axis=-1)
    log_g = jnp.log1p(-(2.0 ** (-5.0 - jnp.arange(N_HEADS, dtype=F32))))

    h = x.reshape(t, d)

    w1 = mlp_w1.astype(BF16)
    w2 = mlp_w2.astype(BF16)
    n_main = 7 * ret_w
    wf = jnp.pad(even_w_in[0, :, n_main:], ((0, 0), (0, LANES - N_HEADS))).astype(BF16)
    z, ff = _norm_proj(h, row(norm_mix_g[0]), even_w_in.astype(BF16), 0, n_main, wf)
    b_f = jnp.pad(even_b_f[0].astype(F32), (0, LANES - N_HEADS)).reshape(1, LANES)
    fbias = _forget_bias(ff, b_f, batch, seq)
    ret = _retention(z, log_g, cos, sin_signed, row(even_ret_gn[0]), batch, seq)
    fox = _fox_attention(z, fbias, batch, seq)
    h = _proj_residual([ret, fox], even_w_out.astype(BF16), 0, h)
    h = _mlp(h, row(norm_mlp_g[0]), w1, w2, 0)

    lambda_init = 0.8 - 0.6 * math.exp(-0.3 * 1)
    z = _norm_proj(h, row(norm_mix_g[1]), odd_w_in.astype(BF16), 0, odd_w_in.shape[2])
    att = _diff_attention(z, row(odd_lambda_q1[0]), row(odd_lambda_k1[0]),
                          row(odd_lambda_q2[0]), row(odd_lambda_k2[0]),
                          row(odd_subln_g[0]), lambda_init, batch, seq)
    h = _proj_residual([att], odd_w_out.astype(BF16), 0, h)
    h = _mlp(h, row(norm_mlp_g[1]), w1, w2, 1, row(final_g))
    return h.reshape(batch, seq, d)
```

```python
import functools
import math

import jax
import jax.numpy as jnp
from jax import lax
from jax.experimental import pallas as pl
from jax.experimental.pallas import tpu as pltpu

F32 = jnp.float32
BF16 = jnp.bfloat16

CHUNK = 64
CHUNK_SHIFT = 6
HEAD_DIM = 128
N_HEADS = 8
ROPE_BASE = 10000.0
RMS_EPS = 1e-6
GN_EPS = 1e-5
LOG2E = 1.4426950408889634
LANES = 128
SUBLANES = 8
MIB = 1024 * 1024

PROJ_TM = 1024
PROJ_TN = 1024
MLP_TM = 512
MLP_TF = 1024
RET_ROWS = 256
ATT_T = 512
CUMSUM_ROWS = 256


def _params(semantics, vmem_mib):
    return pltpu.CompilerParams(dimension_semantics=semantics,
                                vmem_limit_bytes=vmem_mib * MIB)


def _dot(a, b):
    return jnp.dot(a, b, preferred_element_type=F32)


def _dot_nt(a, b):
    return lax.dot_general(a, b, (((1,), (1,)), ((), ())), preferred_element_type=F32)


def _dot_tn(a, b):
    return lax.dot_general(a, b, (((0,), (0,)), ((), ())), preferred_element_type=F32)


def _rms_normalize(x, g):
    ms = jnp.mean(x * x, axis=-1, keepdims=True)
    return (x * lax.rsqrt(ms + RMS_EPS)) * g


def _slab_reduce(x, op):
    rows, n = x.shape
    return op(x.reshape(rows // SUBLANES, SUBLANES, n), axis=0)


def _norm_proj_kernel(x_ref, g_ref, w_ref, *rest, with_forget):
    if with_forget:
        wf_ref, o_ref, f_ref, xn_ref = rest
    else:
        o_ref, xn_ref = rest

    @pl.when(pl.program_id(1) == 0)
    def _():
        xn_ref[...] = _rms_normalize(x_ref[...], g_ref[...]).astype(BF16)
        if with_forget:
            f_ref[...] = _dot(xn_ref[...], wf_ref[...])

    o_ref[...] = _dot(xn_ref[...], w_ref[...]).astype(o_ref.dtype)


def _norm_proj(x, g, w, layer, n, wf=None):
    t, d = x.shape
    tm, tn = PROJ_TM, PROJ_TN
    with_forget = wf is not None
    in_specs = [
        pl.BlockSpec((tm, d), lambda i, j: (i, 0)),
        pl.BlockSpec((1, d), lambda i, j: (0, 0)),
        pl.BlockSpec((None, d, tn), lambda i, j: (layer, 0, j)),
    ]
    out_shape = [jax.ShapeDtypeStruct((t, n), BF16)]
    out_specs = [pl.BlockSpec((tm, tn), lambda i, j: (i, j))]
    args = [x, g, w]
    if with_forget:
        in_specs.append(pl.BlockSpec((d, LANES), lambda i, j: (0, 0)))
        out_shape.append(jax.ShapeDtypeStruct((t, LANES), F32))
        out_specs.append(pl.BlockSpec((tm, LANES), lambda i, j: (i, 0)))
        args.append(wf)
    out = pl.pallas_call(
        functools.partial(_norm_proj_kernel, with_forget=with_forget),
        grid=(t // tm, n // tn),
        in_specs=in_specs,
        out_specs=out_specs,
        out_shape=out_shape,
        scratch_shapes=[pltpu.VMEM((tm, d), BF16)],
        compiler_params=_params(("parallel", "arbitrary"), 56),
        name="norm_proj_forget" if with_forget else "norm_proj",
    )(*args)
    return out if with_forget else out[0]


def _rotate(x, cos, sin_signed):
    return x * cos + pltpu.roll(x, HEAD_DIM // 2, 1) * sin_signed


def _retention_kernel(lg_ref, q_ref, k_ref, v_ref, gate_ref, cos_ref, sin_ref, gn_ref, o_ref):
    seq = q_ref.shape[0]
    rows = RET_ROWS
    lg = lg_ref[pl.program_id(1)]

    row = lax.broadcasted_iota(jnp.int32, (rows, HEAD_DIM), 0).astype(F32)
    xi = jnp.exp(lg * (row + 1.0))
    zeta = jnp.exp(lg * (rows - 1.0 - row))
    ti = lax.broadcasted_iota(jnp.int32, (rows, rows), 0)
    si = lax.broadcasted_iota(jnp.int32, (rows, rows), 1)
    visible = (si >> CHUNK_SHIFT) <= (ti >> CHUNK_SHIFT)
    decay = jnp.where(visible, jnp.exp(lg * jnp.abs(ti - si).astype(F32)), 0.0)
    block_decay = jnp.exp(jnp.full((1, HEAD_DIM), lg * rows, F32))
    gn = gn_ref[...]

    state = jnp.zeros((HEAD_DIM, HEAD_DIM), F32)
    for r in range(seq // rows):
        sl = slice(r * rows, (r + 1) * rows)
        cos = cos_ref[sl, :]
        sin = sin_ref[sl, :]
        q = _rotate(q_ref[sl, :].astype(F32), cos, sin) * (HEAD_DIM ** -0.5)
        k = _rotate(k_ref[sl, :].astype(F32), cos, sin)
        v = v_ref[sl, :]
        scores = _dot_nt(q.astype(BF16), k.astype(BF16)) * decay
        out = _dot(scores.astype(BF16), v) + _dot((q * xi).astype(BF16), state.astype(BF16))
        state = state * block_decay + _dot_tn((k * zeta).astype(BF16), v)

        ms = jnp.mean(out * out, axis=-1, keepdims=True)
        y = out * lax.rsqrt(ms + GN_EPS) * gn
        gate = gate_ref[sl, :].astype(F32)
        o_ref[sl, :] = (jax.nn.silu(gate) * y).astype(o_ref.dtype)


def _retention(z, log_g, cos, sin_signed, gn, batch, seq):
    t = z.shape[0]
    col = lambda off: pl.BlockSpec((seq, HEAD_DIM), lambda b, h: (b, off + h))
    tab = pl.BlockSpec((seq, HEAD_DIM), lambda b, h: (0, 0))
    return pl.pallas_call(
        _retention_kernel,
        grid=(batch, N_HEADS),
        in_specs=[
            pl.BlockSpec(memory_space=pltpu.SMEM),
            col(0), col(N_HEADS), col(2 * N_HEADS), col(3 * N_HEADS),
            tab, tab,
            pl.BlockSpec((1, HEAD_DIM), lambda b, h: (0, h)),
        ],
        out_specs=pl.BlockSpec((seq, HEAD_DIM), lambda b, h: (b, h)),
        out_shape=jax.ShapeDtypeStruct((t, N_HEADS * HEAD_DIM), BF16),
        compiler_params=_params(("parallel", "parallel"), 32),
        name="retention",
    )(log_g, z, z, z, z, cos, sin_signed, gn)


def _forget_bias_kernel(ff_ref, b_ref, tri_ref, o_ref):
    seq = ff_ref.shape[0]
    rows = tri_ref.shape[0]
    lane = lax.broadcasted_iota(jnp.int32, (rows, LANES), 1)
    carry = jnp.zeros((1, LANES), F32)
    for blk in range(seq // rows):
        sl = slice(blk * rows, (blk + 1) * rows)
        log_f = jax.nn.log_sigmoid(ff_ref[sl, :] + b_ref[...])
        c = jnp.dot(tri_ref[...], log_f, precision=lax.Precision.HIGHEST,
                    preferred_element_type=F32) + carry
        carry = c[rows - 1:rows, :]
        for h in range(N_HEADS):
            x = jnp.broadcast_to(c[:, h:h + 1], (rows, LANES)) * (-LOG2E)
            hi = x.astype(BF16).astype(F32)
            mid = (x - hi).astype(BF16).astype(F32)
            lo = x - hi - mid
            pieces = jnp.where(lane == 0, hi, jnp.where(lane == 1, mid,
                                                       jnp.where(lane == 2, lo, 0.0)))
            o_ref[h, sl, :] = pieces.astype(o_ref.dtype)


def _forget_bias(ff, b_f, batch, seq):
    rows = CUMSUM_ROWS
    tri = (jnp.arange(rows)[None, :] <= jnp.arange(rows)[:, None]).astype(F32)
    out = pl.pallas_call(
        _forget_bias_kernel,
        grid=(batch,),
        in_specs=[
            pl.BlockSpec((seq, LANES), lambda b: (b, 0)),
            pl.BlockSpec((1, LANES), lambda b: (0, 0)),
            pl.BlockSpec((rows, rows), lambda b: (0, 0)),
        ],
        out_specs=pl.BlockSpec((None, N_HEADS, seq, LANES), lambda b: (b, 0, 0, 0)),
        out_shape=jax.ShapeDtypeStruct((batch, N_HEADS, seq, LANES), BF16),
        compiler_params=_params(("parallel",), 32),
        name="forget_bias",
    )(ff, b_f, tri)
    return out.reshape(batch * N_HEADS * seq, LANES)


def _pipeline_blocks(blocks):
    for task in blocks[0][0]:
        task()
    for n, (_, value_tasks) in enumerate(blocks):
        score_tasks = blocks[n + 1][0] if n + 1 < len(blocks) else []
        done = 0
        for idx, task in enumerate(value_tasks):
            task()
            while done < len(score_tasks) and done * len(value_tasks) < (idx + 1) * len(score_tasks):
                score_tasks[done]()
                done += 1


def _diag_tile_max(s, n_full, band_rows, band_mask):
    parts = []
    for j in range(s.shape[1] // LANES):
        lanes = slice(j * LANES, (j + 1) * LANES)
        nf = n_full(j)
        band = jnp.where(band_mask, s[nf:nf + band_rows, lanes], -jnp.inf)
        mx = _slab_reduce(band, jnp.max)
        if nf:
            mx = jnp.maximum(mx, _slab_reduce(s[:nf, lanes], jnp.max))
        parts.append(mx)
    return jnp.concatenate(parts, axis=1)


def _diag_tile_exp(s_ref, row0, m, n_full, band_rows, band_mask):
    t = s_ref.shape[1]
    strips, sums = [], []
    for j in range(t // LANES):
        lanes = slice(j * LANES, (j + 1) * LANES)
        nf = n_full(j)
        nv = nf + band_rows
        pj = jnp.exp2(s_ref[row0:row0 + nv, lanes] - m[:, lanes])
        band = jnp.where(band_mask, pj[nf:, :], 0.0)
        total = _slab_reduce(band, jnp.sum)
        pieces = [band.astype(BF16)]
        if nf:
            total = total + _slab_reduce(pj[:nf, :], jnp.sum)
            pieces.insert(0, pj[:nf, :].astype(BF16))
        if nv < t:
            pieces.append(jnp.zeros((t - nv, LANES), BF16))
        strips.append(jnp.concatenate(pieces, axis=0))
        sums.append(total)
    return jnp.concatenate(strips, axis=1), jnp.concatenate(sums, axis=1)


def _fox_kernel(q_ref, k_ref, v_ref, fb_ref, o_ref, vt_ref, s_ref):
    seq = q_ref.shape[0]
    t = ATT_T
    vt_ref[...] = v_ref[...].astype(F32).T.astype(BF16)
    lane = lax.broadcasted_iota(jnp.int32, (t, HEAD_DIM), 1)
    ones3 = jnp.where(lane < 3, 1.0, 0.0).astype(BF16)
    band_mask = (lax.broadcasted_iota(jnp.int32, (LANES, LANES), 0)
                 <= lax.broadcasted_iota(jnp.int32, (LANES, LANES), 1))
    n_full = lambda j: j * LANES
    qscale = (HEAD_DIM ** -0.5) * LOG2E

    def query_block(i):
        rows_i = slice(i * t, (i + 1) * t)
        sbuf = s_ref.at[i % 2]
        st = {"m_run": None, "l_run": jnp.zeros((SUBLANES, t), F32),
              "acc": jnp.zeros((HEAD_DIM, t), F32)}

        def score_task(c):
            rows_c = slice(c * t, (c + 1) * t)
            if c == 0:
                qs = (q_ref[rows_i, :].astype(F32) * qscale).astype(BF16)
                st["qa"] = jnp.concatenate([qs, ones3], axis=1)
            ka = jnp.concatenate([k_ref[rows_c, :], fb_ref[rows_c, :]], axis=1)
            s = _dot_nt(ka, st["qa"])
            sbuf[rows_c, :] = s
            if c == i:
                tile_max = _diag_tile_max(s, n_full, LANES, band_mask)
            else:
                tile_max = _slab_reduce(s, jnp.max)
            st["m_run"] = tile_max if c == 0 else jnp.maximum(st["m_run"], tile_max)

        def value_task(c):
            rows_c = slice(c * t, (c + 1) * t)
            if c == 0:
                st["m"] = jnp.max(st["m_run"], axis=0, keepdims=True)
            if c == i:
                p, sums = _diag_tile_exp(sbuf, c * t, st["m"], n_full, LANES, band_mask)
            else:
                pf = jnp.exp2(sbuf[rows_c, :] - st["m"])
                p, sums = pf.astype(BF16), _slab_reduce(pf, jnp.sum)
            st["l_run"] = st["l_run"] + sums
            st["acc"] = st["acc"] + _dot(vt_ref[:, rows_c], p)
            if c == i:
                l = jnp.sum(st["l_run"], axis=0, keepdims=True)
                o_ref[rows_i, :] = (st["acc"] / l).T.astype(o_ref.dtype)

        tiles = range(i + 1)
        return ([functools.partial(score_task, c) for c in tiles],
                [functools.partial(value_task, c) for c in tiles])

    _pipeline_blocks([query_block(i) for i in range(seq // t)])


def _fox_attention(z, fbias, batch, seq):
    t = z.shape[0]
    col = lambda off: pl.BlockSpec((seq, HEAD_DIM), lambda b, h: (b, off + h))
    return pl.pallas_call(
        _fox_kernel,
        grid=(batch, N_HEADS),
        in_specs=[
            col(4 * N_HEADS), col(5 * N_HEADS), col(6 * N_HEADS),
            pl.BlockSpec((seq, LANES), lambda b, h: (b * N_HEADS + h, 0)),
        ],
        out_specs=pl.BlockSpec((seq, HEAD_DIM), lambda b, h: (b, h)),
        out_shape=jax.ShapeDtypeStruct((t, N_HEADS * HEAD_DIM), BF16),
        scratch_shapes=[pltpu.VMEM((HEAD_DIM, seq), BF16),
                        pltpu.VMEM((2, seq, ATT_T), F32)],
        compiler_params=_params(("parallel", "parallel"), 40),
        name="fox_attention",
    )(z, z, z, fbias)


def _diff_kernel(q_ref, k_ref, v_ref, lq1_ref, lk1_ref, lq2_ref, lk2_ref, g_ref, o_ref,
                 vt_ref, s1_ref, s2_ref, p1_ref, p2_ref, acc_ref, *, lambda_init):
    seq = q_ref.shape[0]
    t = ATT_T
    vt_ref[...] = v_ref[...].astype(F32).T.astype(BF16)
    lam = (jnp.exp(jnp.sum(lq1_ref[...] * lk1_ref[...], axis=-1, keepdims=True))
           - jnp.exp(jnp.sum(lq2_ref[...] * lk2_ref[...], axis=-1, keepdims=True))
           + lambda_init)
    band_mask = lax.broadcasted_iota(jnp.int32, (CHUNK, LANES), 1) >= CHUNK
    n_full = lambda j: j * LANES + CHUNK
    qscale = (HEAD_DIM ** -0.5) * LOG2E
    col_slices = (slice(0, HEAD_DIM), slice(HEAD_DIM, 2 * HEAD_DIM))
    p_refs = (p1_ref, p2_ref)

    def query_block(i):
        rows_i = slice(i * t, (i + 1) * t)
        s_refs = (s1_ref.at[i % 2], s2_ref.at[i % 2])
        zero = jnp.zeros((SUBLANES, t), F32)
        st = {"m_run": [None, None], "m": [None, None], "l_run": [zero, zero]}

        def score_task(mi, c):
            rows_c = slice(c * t, (c + 1) * t)
            if mi == 0 and c == 0:
                st["qs"] = (q_ref[rows_i, :].astype(F32) * qscale).astype(BF16)
            cols = col_slices[mi]
            s = _dot_nt(k_ref[rows_c, cols], st["qs"][:, cols])
            s_refs[mi][rows_c, :] = s
            if c == i:
                tile_max = _diag_tile_max(s, n_full, CHUNK, band_mask)
            else:
                tile_max = _slab_reduce(s, jnp.max)
            st["m_run"][mi] = tile_max if c == 0 else jnp.maximum(st["m_run"][mi], tile_max)

        def exp_task(mi, c):
            rows_c = slice(c * t, (c + 1) * t)
            if c == 0:
                st["m"][mi] = jnp.max(st["m_run"][mi], axis=0, keepdims=True)
            if c == i:
                p, sums = _diag_tile_exp(s_refs[mi], c * t, st["m"][mi], n_full, CHUNK,
                                         band_mask)
            else:
                pf = jnp.exp2(s_refs[mi][rows_c, :] - st["m"][mi])
                p, sums = pf.astype(BF16), _slab_reduce(pf, jnp.sum)
            p_refs[mi][rows_c, :] = p
            st["l_run"][mi] = st["l_run"][mi] + sums

        def value_task(c):
            rows_c = slice(c * t, (c + 1) * t)
            if c == 0:
                st["l1"] = jnp.sum(st["l_run"][0], axis=0, keepdims=True)
                l2 = jnp.sum(st["l_run"][1], axis=0, keepdims=True)
                st["rho"] = (lam * st["l1"] / l2).astype(BF16)
            a = p1_ref[rows_c, :] - p2_ref[rows_c, :] * st["rho"]
            av = _dot(vt_ref[:, rows_c], a)
            acc_ref[...] = av if c == 0 else acc_ref[...] + av
            if c == i:
                o = (acc_ref[...] / st["l1"]).T
                ms = jnp.mean(o * o, axis=-1, keepdims=True)
                y = o * lax.rsqrt(ms + GN_EPS) * g_ref[...] * (1.0 - lambda_init)
                o_ref[rows_i, :] = y.astype(o_ref.dtype)

        tiles = range(i + 1)
        scores = [functools.partial(score_task, mi, c) for mi in (0, 1) for c in tiles]
        values = ([functools.partial(exp_task, mi, c) for mi in (0, 1) for c in tiles]
                  + [functools.partial(value_task, c) for c in tiles])
        return scores, values

    _pipeline_blocks([query_block(i) for i in range(seq // t)])


def _diff_attention(z, lq1, lk1, lq2, lk2, subln_g, lambda_init, batch, seq):
    t = z.shape[0]
    width = 2 * HEAD_DIM
    col = lambda off: pl.BlockSpec((seq, width), lambda b, h: (b, off + h))
    vec = pl.BlockSpec((1, HEAD_DIM), lambda b, h: (0, 0))
    return pl.pallas_call(
        functools.partial(_diff_kernel, lambda_init=lambda_init),
        grid=(batch, N_HEADS),
        in_specs=[
            col(0), col(N_HEADS), col(2 * N_HEADS),
            vec, vec, vec, vec,
            pl.BlockSpec((1, width), lambda b, h: (0, 0)),
        ],
        out_specs=pl.BlockSpec((seq, width), lambda b, h: (b, h)),
        out_shape=jax.ShapeDtypeStruct((t, N_HEADS * width), BF16),
        scratch_shapes=[pltpu.VMEM((width, seq), BF16),
                        pltpu.VMEM((2, seq, ATT_T), F32),
                        pltpu.VMEM((2, seq, ATT_T), F32),
                        pltpu.VMEM((seq, ATT_T), BF16),
                        pltpu.VMEM((seq, ATT_T), BF16),
                        pltpu.VMEM((width, ATT_T), F32)],
        compiler_params=_params(("parallel", "parallel"), 48),
        name="diff_attention",
    )(z, z, z, lq1, lk1, lq2, lk2, subln_g)


def _proj_residual_kernel(*refs, n_parts):
    a_refs = refs[:n_parts]
    w_refs = refs[n_parts:2 * n_parts]
    res_ref, o_ref = refs[2 * n_parts:]
    acc = res_ref[...]
    for a_ref, w_ref in zip(a_refs, w_refs):
        acc = acc + _dot(a_ref[...], w_ref[...])
    o_ref[...] = acc


def _proj_residual(parts, w, layer, res):
    t, n = res.shape
    tm, tn = PROJ_TM, PROJ_TN
    kp = parts[0].shape[1]
    n_parts = len(parts)
    a_specs = [pl.BlockSpec((tm, kp), lambda i, j: (i, 0)) for _ in parts]
    w_specs = [pl.BlockSpec((None, kp, tn),
                            functools.partial(lambda i, j, p: (layer, p, j), p=p))
               for p in range(n_parts)]
    return pl.pallas_call(
        functools.partial(_proj_residual_kernel, n_parts=n_parts),
        grid=(t // tm, n // tn),
        in_specs=a_specs + w_specs + [pl.BlockSpec((tm, tn), lambda i, j: (i, j))],
        out_specs=pl.BlockSpec((tm, tn), lambda i, j: (i, j)),
        out_shape=jax.ShapeDtypeStruct((t, n), F32),
        compiler_params=_params(("parallel", "parallel"), 48),
        name="proj_residual",
    )(*parts, *([w] * n_parts), res)


def _mlp_kernel(h_ref, g_ref, w1_ref, w2_ref, *rest, with_final):
    if with_final:
        fg_ref, o_ref, xn_ref, acc_ref = rest
    else:
        o_ref, xn_ref, acc_ref = rest
    f = pl.program_id(1)

    @pl.when(f == 0)
    def _():
        xn_ref[...] = _rms_normalize(h_ref[...], g_ref[...]).astype(BF16)
        acc_ref[...] = jnp.zeros_like(acc_ref)

    a = jnp.maximum(_dot(xn_ref[...], w1_ref[...]), 0.0)
    acc_ref[...] += _dot((a * a).astype(BF16), w2_ref[...])

    @pl.when(f == pl.num_programs(1) - 1)
    def _():
        out = h_ref[...] + acc_ref[...]
        if with_final:
            out = _rms_normalize(out, fg_ref[...])
        o_ref[...] = out


def _mlp(h, g, w1, w2, layer, final_g=None):
    t, d = h.shape
    dff = w1.shape[2]
    tm, tf = MLP_TM, MLP_TF
    with_final = final_g is not None
    in_specs = [
        pl.BlockSpec((tm, d), lambda i, f: (i, 0)),
        pl.BlockSpec((1, d), lambda i, f: (0, 0)),
        pl.BlockSpec((None, d, tf), lambda i, f: (layer, 0, f)),
        pl.BlockSpec((None, tf, d), lambda i, f: (layer, f, 0)),
    ]
    args = [h, g, w1, w2]
    if with_final:
        in_specs.append(pl.BlockSpec((1, d), lambda i, f: (0, 0)))
        args.append(final_g)
    return pl.pallas_call(
        functools.partial(_mlp_kernel, with_final=with_final),
        grid=(t // tm, dff // tf),
        in_specs=in_specs,
        out_specs=pl.BlockSpec((tm, d), lambda i, f: (i, 0)),
        out_shape=jax.ShapeDtypeStruct((t, d), F32),
        scratch_shapes=[pltpu.VMEM((tm, d), BF16), pltpu.VMEM((tm, d), F32)],
        compiler_params=_params(("parallel", "arbitrary"), 56),
        name="mlp_final" if with_final else "mlp",
    )(*args)


def kernel(x, norm_mix_g, norm_mlp_g, even_w_in, even_b_f, even_ret_gn, even_w_out,
           odd_w_in, odd_lambda_q1, odd_lambda_k1, odd_lambda_q2, odd_lambda_k2,
           odd_subln_g, odd_w_out, mlp_w1, mlp_w2, final_g):
    batch, seq, d = x.shape
    t = batch * seq
    ret_w = N_HEADS * HEAD_DIM
    row = lambda v: v.reshape(1, -1).astype(F32)

    half = HEAD_DIM // 2
    inv = ROPE_BASE ** (-jnp.arange(half, dtype=F32) / half)
    ang = jnp.arange(seq, dtype=F32)[:, None] * inv[None, :]
    cos = jnp.concatenate([jnp.cos(ang), jnp.cos(ang)], axis=-1)
    sin_signed = jnp.concatenate([-jnp.sin(ang), jnp.sin(ang)], axis=-1)
    log_g = jnp.log1p(-(2.0 ** (-5.0 - jnp.arange(N_HEADS, dtype=F32))))

    h = x.reshape(t, d)

    w1 = mlp_w1.astype(BF16)
    w2 = mlp_w2.astype(BF16)
    n_main = 7 * ret_w
    wf = jnp.pad(even_w_in[0, :, n_main:], ((0, 0), (0, LANES - N_HEADS))).astype(BF16)
    z, ff = _norm_proj(h, row(norm_mix_g[0]), even_w_in.astype(BF16), 0, n_main, wf)
    b_f = jnp.pad(even_b_f[0].astype(F32), (0, LANES - N_HEADS)).reshape(1, LANES)
    fbias = _forget_bias(ff, b_f, batch, seq)
    ret = _retention(z, log_g, cos, sin_signed, row(even_ret_gn[0]), batch, seq)
    fox = _fox_attention(z, fbias, batch, seq)
    h = _proj_residual([ret, fox], even_w_out.astype(BF16), 0, h)
    h = _mlp(h, row(norm_mlp_g[0]), w1, w2, 0)

    lambda_init = 0.8 - 0.6 * math.exp(-0.3 * 1)
    z = _norm_proj(h, row(norm_mix_g[1]), odd_w_in.astype(BF16), 0, odd_w_in.shape[2])
    att = _diff_attention(z, row(odd_lambda_q1[0]), row(odd_lambda_k1[0]),
                          row(odd_lambda_q2[0]), row(odd_lambda_k2[0]),
                          row(odd_subln_g[0]), lambda_init, batch, seq)
    h = _proj_residual([att], odd_w_out.astype(BF16), 0, h)
    h = _mlp(h, row(norm_mlp_g[1]), w1, w2, 1, row(final_g))
    return h.reshape(batch, seq, d)
```

```python
import functools
import math

import jax
import jax.numpy as jnp
from jax import lax
from jax.experimental import pallas as pl
from jax.experimental.pallas import tpu as pltpu

F32 = jnp.float32
BF16 = jnp.bfloat16

CHUNK = 64
CHUNK_SHIFT = 6
HEAD_DIM = 128
N_HEADS = 8
ROPE_BASE = 10000.0
RMS_EPS = 1e-6
GN_EPS = 1e-5
LOG2E = 1.4426950408889634
LANES = 128
SUBLANES = 8
MIB = 1024 * 1024

PROJ_TM = 1024
PROJ_TN = 1024
OUT_TM = 512
MLP_TM = 512
MLP_TF = 1024
RET_ROWS = 256
ATT_T = 512
CUMSUM_ROWS = 256


def _params(semantics, vmem_mib):
    return pltpu.CompilerParams(dimension_semantics=semantics,
                                vmem_limit_bytes=vmem_mib * MIB)


def _dot(a, b):
    return jnp.dot(a, b, preferred_element_type=F32)


def _dot_nt(a, b):
    return lax.dot_general(a, b, (((1,), (1,)), ((), ())), preferred_element_type=F32)


def _dot_tn(a, b):
    return lax.dot_general(a, b, (((0,), (0,)), ((), ())), preferred_element_type=F32)


def _rms_normalize(x, g):
    ms = jnp.mean(x * x, axis=-1, keepdims=True)
    return (x * lax.rsqrt(ms + RMS_EPS)) * g


def _slab_reduce(x, op):
    rows, n = x.shape
    return op(x.reshape(rows // SUBLANES, SUBLANES, n), axis=0)


def _norm_proj_kernel(x_ref, g_ref, w_ref, *rest, with_forget):
    if with_forget:
        wf_ref, o_ref, f_ref, xn_ref = rest
    else:
        o_ref, xn_ref = rest

    @pl.when(pl.program_id(1) == 0)
    def _():
        xn_ref[...] = _rms_normalize(x_ref[...], g_ref[...]).astype(BF16)
        if with_forget:
            f_ref[...] = _dot(xn_ref[...], wf_ref[...])

    o_ref[...] = _dot(xn_ref[...], w_ref[...]).astype(o_ref.dtype)


def _norm_proj(x, g, w, layer, n, wf=None):
    t, d = x.shape
    tm, tn = PROJ_TM, PROJ_TN
    with_forget = wf is not None
    in_specs = [
        pl.BlockSpec((tm, d), lambda i, j: (i, 0)),
        pl.BlockSpec((1, d), lambda i, j: (0, 0)),
        pl.BlockSpec((None, d, tn), lambda i, j: (layer, 0, j)),
    ]
    out_shape = [jax.ShapeDtypeStruct((t, n), BF16)]
    out_specs = [pl.BlockSpec((tm, tn), lambda i, j: (i, j))]
    args = [x, g, w]
    if with_forget:
        in_specs.append(pl.BlockSpec((d, LANES), lambda i, j: (0, 0)))
        out_shape.append(jax.ShapeDtypeStruct((t, LANES), F32))
        out_specs.append(pl.BlockSpec((tm, LANES), lambda i, j: (i, 0)))
        args.append(wf)
    out = pl.pallas_call(
        functools.partial(_norm_proj_kernel, with_forget=with_forget),
        grid=(t // tm, n // tn),
        in_specs=in_specs,
        out_specs=out_specs,
        out_shape=out_shape,
        scratch_shapes=[pltpu.VMEM((tm, d), BF16)],
        compiler_params=_params(("parallel", "arbitrary"), 56),
        name="norm_proj_forget" if with_forget else "norm_proj",
    )(*args)
    return out if with_forget else out[0]


def _rotate(x, cos, sin_signed):
    return x * cos + pltpu.roll(x, HEAD_DIM // 2, 1) * sin_signed


def _retention_kernel(lg_ref, q_ref, k_ref, v_ref, gate_ref, cos_ref, sin_ref, gn_ref, o_ref):
    seq = q_ref.shape[0]
    rows = RET_ROWS
    lg = lg_ref[pl.program_id(1)]

    row = lax.broadcasted_iota(jnp.int32, (rows, HEAD_DIM), 0).astype(F32)
    xi = jnp.exp(lg * (row + 1.0))
    zeta = jnp.exp(lg * (rows - 1.0 - row))
    ti = lax.broadcasted_iota(jnp.int32, (rows, rows), 0)
    si = lax.broadcasted_iota(jnp.int32, (rows, rows), 1)
    visible = (si >> CHUNK_SHIFT) <= (ti >> CHUNK_SHIFT)
    decay = jnp.where(visible, jnp.exp(lg * jnp.abs(ti - si).astype(F32)), 0.0)
    block_decay = jnp.exp(jnp.full((1, HEAD_DIM), lg * rows, F32))
    gn = gn_ref[...]

    state = jnp.zeros((HEAD_DIM, HEAD_DIM), F32)
    for r in range(seq // rows):
        sl = slice(r * rows, (r + 1) * rows)
        cos = cos_ref[sl, :]
        sin = sin_ref[sl, :]
        q = _rotate(q_ref[sl, :].astype(F32), cos, sin) * (HEAD_DIM ** -0.5)
        k = _rotate(k_ref[sl, :].astype(F32), cos, sin)
        v = v_ref[sl, :]
        scores = _dot_nt(q.astype(BF16), k.astype(BF16)) * decay
        out = _dot(scores.astype(BF16), v) + _dot((q * xi).astype(BF16), state.astype(BF16))
        state = state * block_decay + _dot_tn((k * zeta).astype(BF16), v)

        ms = jnp.mean(out * out, axis=-1, keepdims=True)
        y = out * lax.rsqrt(ms + GN_EPS) * gn
        gate = gate_ref[sl, :].astype(F32)
        o_ref[sl, :] = (jax.nn.silu(gate) * y).astype(o_ref.dtype)


def _retention(z, log_g, cos, sin_signed, gn, batch, seq):
    t = z.shape[0]
    col = lambda off: pl.BlockSpec((seq, HEAD_DIM), lambda b, h: (b, off + h))
    tab = pl.BlockSpec((seq, HEAD_DIM), lambda b, h: (0, 0))
    return pl.pallas_call(
        _retention_kernel,
        grid=(batch, N_HEADS),
        in_specs=[
            pl.BlockSpec(memory_space=pltpu.SMEM),
            col(0), col(N_HEADS), col(2 * N_HEADS), col(3 * N_HEADS),
            tab, tab,
            pl.BlockSpec((1, HEAD_DIM), lambda b, h: (0, h)),
        ],
        out_specs=pl.BlockSpec((seq, HEAD_DIM), lambda b, h: (b, h)),
        out_shape=jax.ShapeDtypeStruct((t, N_HEADS * HEAD_DIM), BF16),
        compiler_params=_params(("parallel", "parallel"), 32),
        name="retention",
    )(log_g, z, z, z, z, cos, sin_signed, gn)


def _forget_bias_kernel(ff_ref, b_ref, tri_ref, o_ref):
    seq = ff_ref.shape[0]
    rows = tri_ref.shape[0]
    lane = lax.broadcasted_iota(jnp.int32, (rows, LANES), 1)
    carry = jnp.zeros((1, LANES), F32)
    for blk in range(seq // rows):
        sl = slice(blk * rows, (blk + 1) * rows)
        log_f = jax.nn.log_sigmoid(ff_ref[sl, :] + b_ref[...])
        c = jnp.dot(tri_ref[...], log_f, precision=lax.Precision.HIGHEST,
                    preferred_element_type=F32) + carry
        carry = c[rows - 1:rows, :]
        for h in range(N_HEADS):
            x = jnp.broadcast_to(c[:, h:h + 1], (rows, LANES)) * (-LOG2E)
            hi = x.astype(BF16).astype(F32)
            mid = (x - hi).astype(BF16).astype(F32)
            lo = x - hi - mid
            pieces = jnp.where(lane == 0, hi, jnp.where(lane == 1, mid,
                                                       jnp.where(lane == 2, lo, 0.0)))
            o_ref[h, sl, :] = pieces.astype(o_ref.dtype)


def _forget_bias(ff, b_f, batch, seq):
    rows = CUMSUM_ROWS
    tri = (jnp.arange(rows)[None, :] <= jnp.arange(rows)[:, None]).astype(F32)
    out = pl.pallas_call(
        _forget_bias_kernel,
        grid=(batch,),
        in_specs=[
            pl.BlockSpec((seq, LANES), lambda b: (b, 0)),
            pl.BlockSpec((1, LANES), lambda b: (0, 0)),
            pl.BlockSpec((rows, rows), lambda b: (0, 0)),
        ],
        out_specs=pl.BlockSpec((None, N_HEADS, seq, LANES), lambda b: (b, 0, 0, 0)),
        out_shape=jax.ShapeDtypeStruct((batch, N_HEADS, seq, LANES), BF16),
        compiler_params=_params(("parallel",), 32),
        name="forget_bias",
    )(ff, b_f, tri)
    return out.reshape(batch * N_HEADS * seq, LANES)


def _pipeline_blocks(blocks):
    for task in blocks[0][0]:
        task()
    for n, (_, value_tasks) in enumerate(blocks):
        score_tasks = blocks[n + 1][0] if n + 1 < len(blocks) else []
        done = 0
        for idx, task in enumerate(value_tasks):
            task()
            while done < len(score_tasks) and done * len(value_tasks) < (idx + 1) * len(score_tasks):
                score_tasks[done]()
                done += 1


def _diag_tile_max(s, n_full, band_rows, band_mask):
    parts = []
    for j in range(s.shape[1] // LANES):
        lanes = slice(j * LANES, (j + 1) * LANES)
        nf = n_full(j)
        band = jnp.where(band_mask, s[nf:nf + band_rows, lanes], -jnp.inf)
        mx = _slab_reduce(band, jnp.max)
        if nf:
            mx = jnp.maximum(mx, _slab_reduce(s[:nf, lanes], jnp.max))
        parts.append(mx)
    return jnp.concatenate(parts, axis=1)


def _diag_tile_exp(s_ref, row0, m, n_full, band_rows, band_mask):
    t = s_ref.shape[1]
    strips, sums = [], []
    for j in range(t // LANES):
        lanes = slice(j * LANES, (j + 1) * LANES)
        nf = n_full(j)
        nv = nf + band_rows
        pj = jnp.exp2(s_ref[row0:row0 + nv, lanes] - m[:, lanes])
        band = jnp.where(band_mask, pj[nf:, :], 0.0)
        total = _slab_reduce(band, jnp.sum)
        pieces = [band.astype(BF16)]
        if nf:
            total = total + _slab_reduce(pj[:nf, :], jnp.sum)
            pieces.insert(0, pj[:nf, :].astype(BF16))
        if nv < t:
            pieces.append(jnp.zeros((t - nv, LANES), BF16))
        strips.append(jnp.concatenate(pieces, axis=0))
        sums.append(total)
    return jnp.concatenate(strips, axis=1), jnp.concatenate(sums, axis=1)


def _fox_kernel(q_ref, k_ref, v_ref, fb_ref, o_ref, vt_ref, s_ref):
    seq = q_ref.shape[0]
    t = ATT_T
    vt_ref[...] = v_ref[...].astype(F32).T.astype(BF16)
    lane = lax.broadcasted_iota(jnp.int32, (t, HEAD_DIM), 1)
    ones3 = jnp.where(lane < 3, 1.0, 0.0).astype(BF16)
    band_mask = (lax.broadcasted_iota(jnp.int32, (LANES, LANES), 0)
                 <= lax.broadcasted_iota(jnp.int32, (LANES, LANES), 1))
    n_full = lambda j: j * LANES
    qscale = (HEAD_DIM ** -0.5) * LOG2E

    def query_block(i):
        rows_i = slice(i * t, (i + 1) * t)
        sbuf = s_ref.at[i % 2]
        st = {"m_run": None, "l_run": jnp.zeros((SUBLANES, t), F32),
              "acc": jnp.zeros((HEAD_DIM, t), F32)}

        def score_task(c):
            rows_c = slice(c * t, (c + 1) * t)
            if c == 0:
                qs = (q_ref[rows_i, :].astype(F32) * qscale).astype(BF16)
                st["qa"] = jnp.concatenate([qs, ones3], axis=1)
            ka = jnp.concatenate([k_ref[rows_c, :], fb_ref[rows_c, :]], axis=1)
            s = _dot_nt(ka, st["qa"])
            sbuf[rows_c, :] = s
            if c == i:
                tile_max = _diag_tile_max(s, n_full, LANES, band_mask)
            else:
                tile_max = _slab_reduce(s, jnp.max)
            st["m_run"] = tile_max if c == 0 else jnp.maximum(st["m_run"], tile_max)

        def value_task(c):
            rows_c = slice(c * t, (c + 1) * t)
            if c == 0:
                st["m"] = jnp.max(st["m_run"], axis=0, keepdims=True)
            if c == i:
                p, sums = _diag_tile_exp(sbuf, c * t, st["m"], n_full, LANES, band_mask)
            else:
                pf = jnp.exp2(sbuf[rows_c, :] - st["m"])
                p, sums = pf.astype(BF16), _slab_reduce(pf, jnp.sum)
            st["l_run"] = st["l_run"] + sums
            st["acc"] = st["acc"] + _dot(vt_ref[:, rows_c], p)
            if c == i:
                l = jnp.sum(st["l_run"], axis=0, keepdims=True)
                o_ref[rows_i, :] = (st["acc"] / l).T.astype(o_ref.dtype)

        tiles = range(i + 1)
        return ([functools.partial(score_task, c) for c in tiles],
                [functools.partial(value_task, c) for c in tiles])

    _pipeline_blocks([query_block(i) for i in range(seq // t)])


def _fox_attention(z, fbias, batch, seq):
    t = z.shape[0]
    col = lambda off: pl.BlockSpec((seq, HEAD_DIM), lambda b, h: (b, off + h))
    return pl.pallas_call(
        _fox_kernel,
        grid=(batch, N_HEADS),
        in_specs=[
            col(4 * N_HEADS), col(5 * N_HEADS), col(6 * N_HEADS),
            pl.BlockSpec((seq, LANES), lambda b, h: (b * N_HEADS + h, 0)),
        ],
        out_specs=pl.BlockSpec((seq, HEAD_DIM), lambda b, h: (b, h)),
        out_shape=jax.ShapeDtypeStruct((t, N_HEADS * HEAD_DIM), BF16),
        scratch_shapes=[pltpu.VMEM((HEAD_DIM, seq), BF16),
                        pltpu.VMEM((2, seq, ATT_T), F32)],
        compiler_params=_params(("parallel", "parallel"), 40),
        name="fox_attention",
    )(z, z, z, fbias)


def _diff_kernel(q_ref, k_ref, v_ref, lq1_ref, lk1_ref, lq2_ref, lk2_ref, g_ref, o_ref,
                 vt_ref, s1_ref, s2_ref, p1_ref, p2_ref, acc_ref, *, lambda_init):
    seq = q_ref.shape[0]
    t = ATT_T
    vt_ref[...] = v_ref[...].astype(F32).T.astype(BF16)
    lam = (jnp.exp(jnp.sum(lq1_ref[...] * lk1_ref[...], axis=-1, keepdims=True))
           - jnp.exp(jnp.sum(lq2_ref[...] * lk2_ref[...], axis=-1, keepdims=True))
           + lambda_init)
    band_mask = lax.broadcasted_iota(jnp.int32, (CHUNK, LANES), 1) >= CHUNK
    n_full = lambda j: j * LANES + CHUNK
    qscale = (HEAD_DIM ** -0.5) * LOG2E
    col_slices = (slice(0, HEAD_DIM), slice(HEAD_DIM, 2 * HEAD_DIM))
    p_refs = (p1_ref, p2_ref)

    def query_block(i):
        rows_i = slice(i * t, (i + 1) * t)
        s_refs = (s1_ref.at[i % 2], s2_ref.at[i % 2])
        zero = jnp.zeros((SUBLANES, t), F32)
        st = {"m_run": [None, None], "m": [None, None], "l_run": [zero, zero]}

        def score_task(mi, c):
            rows_c = slice(c * t, (c + 1) * t)
            if mi == 0 and c == 0:
                st["qs"] = (q_ref[rows_i, :].astype(F32) * qscale).astype(BF16)
            cols = col_slices[mi]
            s = _dot_nt(k_ref[rows_c, cols], st["qs"][:, cols])
            s_refs[mi][rows_c, :] = s
            if c == i:
                tile_max = _diag_tile_max(s, n_full, CHUNK, band_mask)
            else:
                tile_max = _slab_reduce(s, jnp.max)
            st["m_run"][mi] = tile_max if c == 0 else jnp.maximum(st["m_run"][mi], tile_max)

        def exp_task(mi, c):
            rows_c = slice(c * t, (c + 1) * t)
            if c == 0:
                st["m"][mi] = jnp.max(st["m_run"][mi], axis=0, keepdims=True)
            if c == i:
                p, sums = _diag_tile_exp(s_refs[mi], c * t, st["m"][mi], n_full, CHUNK,
                                         band_mask)
            else:
                pf = jnp.exp2(s_refs[mi][rows_c, :] - st["m"][mi])
                p, sums = pf.astype(BF16), _slab_reduce(pf, jnp.sum)
            p_refs[mi][rows_c, :] = p
            st["l_run"][mi] = st["l_run"][mi] + sums

        def value_task(c):
            rows_c = slice(c * t, (c + 1) * t)
            if c == 0:
                st["l1"] = jnp.sum(st["l_run"][0], axis=0, keepdims=True)
                l2 = jnp.sum(st["l_run"][1], axis=0, keepdims=True)
                st["rho"] = (lam * st["l1"] / l2).astype(BF16)
            a = p1_ref[rows_c, :] - p2_ref[rows_c, :] * st["rho"]
            av = _dot(vt_ref[:, rows_c], a)
            acc_ref[...] = av if c == 0 else acc_ref[...] + av
            if c == i:
                o = (acc_ref[...] / st["l1"]).T
                ms = jnp.mean(o * o, axis=-1, keepdims=True)
                y = o * lax.rsqrt(ms + GN_EPS) * g_ref[...] * (1.0 - lambda_init)
                o_ref[rows_i, :] = y.astype(o_ref.dtype)

        tiles = range(i + 1)
        scores = [functools.partial(score_task, mi, c) for mi in (0, 1) for c in tiles]
        values = ([functools.partial(exp_task, mi, c) for mi in (0, 1) for c in tiles]
                  + [functools.partial(value_task, c) for c in tiles])
        return scores, values

    _pipeline_blocks([query_block(i) for i in range(seq // t)])


def _diff_attention(z, lq1, lk1, lq2, lk2, subln_g, lambda_init, batch, seq):
    t = z.shape[0]
    width = 2 * HEAD_DIM
    col = lambda off: pl.BlockSpec((seq, width), lambda b, h: (b, off + h))
    vec = pl.BlockSpec((1, HEAD_DIM), lambda b, h: (0, 0))
    return pl.pallas_call(
        functools.partial(_diff_kernel, lambda_init=lambda_init),
        grid=(batch, N_HEADS),
        in_specs=[
            col(0), col(N_HEADS), col(2 * N_HEADS),
            vec, vec, vec, vec,
            pl.BlockSpec((1, width), lambda b, h: (0, 0)),
        ],
        out_specs=pl.BlockSpec((seq, width), lambda b, h: (b, h)),
        out_shape=jax.ShapeDtypeStruct((t, N_HEADS * width), BF16),
        scratch_shapes=[pltpu.VMEM((width, seq), BF16),
                        pltpu.VMEM((2, seq, ATT_T), F32),
                        pltpu.VMEM((2, seq, ATT_T), F32),
                        pltpu.VMEM((seq, ATT_T), BF16),
                        pltpu.VMEM((seq, ATT_T), BF16),
                        pltpu.VMEM((width, ATT_T), F32)],
        compiler_params=_params(("parallel", "parallel"), 48),
        name="diff_attention",
    )(z, z, z, lq1, lk1, lq2, lk2, subln_g)


def _proj_residual_kernel(*refs, n_parts):
    a_refs = refs[:n_parts]
    w_refs = refs[n_parts:2 * n_parts]
    res_ref, o_ref = refs[2 * n_parts:]
    acc = res_ref[...]
    for a_ref, w_ref in zip(a_refs, w_refs):
        acc = acc + _dot(a_ref[...], w_ref[...])
    o_ref[...] = acc


def _proj_residual(parts, w, layer, res):
    t, n = res.shape
    tm = OUT_TM
    kp = parts[0].shape[1]
    n_parts = len(parts)
    a_specs = [pl.BlockSpec((tm, kp), lambda i: (i, 0)) for _ in parts]
    w_specs = [pl.BlockSpec((None, kp, n), functools.partial(lambda i, p: (layer, p, 0), p=p))
               for p in range(n_parts)]
    return pl.pallas_call(
        functools.partial(_proj_residual_kernel, n_parts=n_parts),
        grid=(t // tm,),
        in_specs=a_specs + w_specs + [pl.BlockSpec((tm, n), lambda i: (i, 0))],
        out_specs=pl.BlockSpec((tm, n), lambda i: (i, 0)),
        out_shape=jax.ShapeDtypeStruct((t, n), F32),
        compiler_params=_params(("parallel",), 48),
        name="proj_residual",
    )(*parts, *([w] * n_parts), res)


def _mlp_kernel(h_ref, hn_ref, g_ref, w1_ref, w2_ref, *rest, with_final, n_steps):
    if with_final:
        fg_ref, o_ref, xn_a_ref, xn_b_ref, acc_ref = rest
    else:
        o_ref, xn_a_ref, xn_b_ref, acc_ref = rest
    i = pl.program_id(0)
    f = pl.program_id(1)
    chunk = h_ref.shape[0] // n_steps

    @pl.when((i == 0) & (f == 0))
    def _():
        xn_a_ref[...] = _rms_normalize(h_ref[...], g_ref[...]).astype(BF16)

    def step(xn_ref, next_ref):
        rows = pl.ds(pl.multiple_of(f * chunk, chunk), chunk)
        next_ref[rows, :] = _rms_normalize(hn_ref[rows, :], g_ref[...]).astype(BF16)
        a = jnp.maximum(_dot(xn_ref[...], w1_ref[...]), 0.0)
        prev = jnp.where(f == 0, 0.0, acc_ref[...])
        acc_ref[...] = prev + _dot((a * a).astype(BF16), w2_ref[...])

    parity = lax.rem(i, 2)
    pl.when(parity == 0)(lambda: step(xn_a_ref, xn_b_ref))
    pl.when(parity == 1)(lambda: step(xn_b_ref, xn_a_ref))

    @pl.when(f == n_steps - 1)
    def _():
        out = h_ref[...] + acc_ref[...]
        if with_final:
            out = _rms_normalize(out, fg_ref[...])
        o_ref[...] = out


def _mlp(h, g, w1, w2, layer, final_g=None):
    t, d = h.shape
    dff = w1.shape[2]
    tm, tf = MLP_TM, MLP_TF
    with_final = final_g is not None
    n_tiles = t // tm
    n_steps = dff // tf
    in_specs = [
        pl.BlockSpec((tm, d), lambda i, f: (i, 0)),
        pl.BlockSpec((tm, d), lambda i, f: (jnp.minimum(i + 1, n_tiles - 1), 0)),
        pl.BlockSpec((1, d), lambda i, f: (0, 0)),
        pl.BlockSpec((None, d, tf), lambda i, f: (layer, 0, f)),
        pl.BlockSpec((None, tf, d), lambda i, f: (layer, f, 0)),
    ]
    args = [h, h, g, w1, w2]
    if with_final:
        in_specs.append(pl.BlockSpec((1, d), lambda i, f: (0, 0)))
        args.append(final_g)
    return pl.pallas_call(
        functools.partial(_mlp_kernel, with_final=with_final, n_steps=n_steps),
        grid=(n_tiles, n_steps),
        in_specs=in_specs,
        out_specs=pl.BlockSpec((tm, d), lambda i, f: (i, 0)),
        out_shape=jax.ShapeDtypeStruct((t, d), F32),
        scratch_shapes=[pltpu.VMEM((tm, d), BF16), pltpu.VMEM((tm, d), BF16),
                        pltpu.VMEM((tm, d), F32)],
        compiler_params=_params(("arbitrary", "arbitrary"), 58),
        name="mlp_final" if with_final else "mlp",
    )(*args)


def kernel(x, norm_mix_g, norm_mlp_g, even_w_in, even_b_f, even_ret_gn, even_w_out,
           odd_w_in, odd_lambda_q1, odd_lambda_k1, odd_lambda_q2, odd_lambda_k2,
           odd_subln_g, odd_w_out, mlp_w1, mlp_w2, final_g):
    batch, seq, d = x.shape
    t = batch * seq
    ret_w = N_HEADS * HEAD_DIM
    row = lambda v: v.reshape(1, -1).astype(F32)

    half = HEAD_DIM // 2
    inv = ROPE_BASE ** (-jnp.arange(half, dtype=F32) / half)
    ang = jnp.arange(seq, dtype=F32)[:, None] * inv[None, :]
    cos = jnp.concatenate([jnp.cos(ang), jnp.cos(ang)], axis=-1)
    sin_signed = jnp.concatenate([-jnp.sin(ang), jnp.sin(ang)], axis=-1)
    log_g = jnp.log1p(-(2.0 ** (-5.0 - jnp.arange(N_HEADS, dtype=F32))))

    h = x.reshape(t, d)

    w1 = mlp_w1.astype(BF16)
    w2 = mlp_w2.astype(BF16)
    n_main = 7 * ret_w
    wf = jnp.pad(even_w_in[0, :, n_main:], ((0, 0), (0, LANES - N_HEADS))).astype(BF16)
    z, ff = _norm_proj(h, row(norm_mix_g[0]), even_w_in.astype(BF16), 0, n_main, wf)
    b_f = jnp.pad(even_b_f[0].astype(F32), (0, LANES - N_HEADS)).reshape(1, LANES)
    fbias = _forget_bias(ff, b_f, batch, seq)
    ret = _retention(z, log_g, cos, sin_signed, row(even_ret_gn[0]), batch, seq)
    fox = _fox_attention(z, fbias, batch, seq)
    h = _proj_residual([ret, fox], even_w_out.astype(BF16), 0, h)
    h = _mlp(h, row(norm_mlp_g[0]), w1, w2, 0)

    lambda_init = 0.8 - 0.6 * math.exp(-0.3 * 1)
    z = _norm_proj(h, row(norm_mix_g[1]), odd_w_in.astype(BF16), 0, odd_w_in.shape[2])
    att = _diff_attention(z, row(odd_lambda_q1[0]), row(odd_lambda_k1[0]),
                          row(odd_lambda_q2[0]), row(odd_lambda_k2[0]),
                          row(odd_subln_g[0]), lambda_init, batch, seq)
    h = _proj_residual([att], odd_w_out.astype(BF16), 0, h)
    h = _mlp(h, row(norm_mlp_g[1]), w1, w2, 1, row(final_g))
    return h.reshape(batch, seq, d)
```

```python
import functools
import math

import jax
import jax.numpy as jnp
from jax import lax
from jax.experimental import pallas as pl
from jax.experimental.pallas import tpu as pltpu

F32 = jnp.float32
BF16 = jnp.bfloat16

CHUNK = 64
CHUNK_SHIFT = 6
HEAD_DIM = 128
N_HEADS = 8
ROPE_BASE = 10000.0
RMS_EPS = 1e-6
GN_EPS = 1e-5
LOG2E = 1.4426950408889634
LANES = 128
SUBLANES = 8
MIB = 1024 * 1024

MXU_COLS = 256
PROJ_TM = 1024
PROJ_TN_MAX = 2048
OUT_TM = 512
MLP_TM = 512
MLP_TF = 1024
RET_ROWS = 256
ATT_T = 512
CUMSUM_ROWS = 256


def _params(semantics, vmem_mib):
    return pltpu.CompilerParams(dimension_semantics=semantics,
                                vmem_limit_bytes=vmem_mib * MIB)


def _dot(a, b):
    return jnp.dot(a, b, preferred_element_type=F32)


def _dot_nt(a, b):
    return lax.dot_general(a, b, (((1,), (1,)), ((), ())), preferred_element_type=F32)


def _dot_tn(a, b):
    return lax.dot_general(a, b, (((0,), (0,)), ((), ())), preferred_element_type=F32)


def _rms_normalize(x, g):
    ms = jnp.mean(x * x, axis=-1, keepdims=True)
    return (x * lax.rsqrt(ms + RMS_EPS)) * g


def _slab_reduce(x, op):
    rows, n = x.shape
    return op(x.reshape(rows // SUBLANES, SUBLANES, n), axis=0)


def _norm_proj_kernel(x_ref, g_ref, w_ref, *rest, with_forget):
    if with_forget:
        wf_ref, o_ref, f_ref, xn_ref = rest
    else:
        o_ref, xn_ref = rest

    @pl.when(pl.program_id(1) == 0)
    def _():
        xn_ref[...] = _rms_normalize(x_ref[...], g_ref[...]).astype(BF16)
        if with_forget:
            f_ref[...] = _dot(xn_ref[...], wf_ref[...])

    o_ref[...] = _dot(xn_ref[...], w_ref[...]).astype(o_ref.dtype)


def _norm_proj(x, g, w, layer, n, wf=None):
    t, d = x.shape
    tm = PROJ_TM
    tn = max(c for c in range(MXU_COLS, PROJ_TN_MAX + 1, MXU_COLS) if n % c == 0)
    with_forget = wf is not None
    in_specs = [
        pl.BlockSpec((tm, d), lambda i, j: (i, 0)),
        pl.BlockSpec((1, d), lambda i, j: (0, 0)),
        pl.BlockSpec((None, d, tn), lambda i, j: (layer, 0, j)),
    ]
    out_shape = [jax.ShapeDtypeStruct((t, n), BF16)]
    out_specs = [pl.BlockSpec((tm, tn), lambda i, j: (i, j))]
    args = [x, g, w]
    if with_forget:
        in_specs.append(pl.BlockSpec((d, LANES), lambda i, j: (0, 0)))
        out_shape.append(jax.ShapeDtypeStruct((t, LANES), F32))
        out_specs.append(pl.BlockSpec((tm, LANES), lambda i, j: (i, 0)))
        args.append(wf)
    out = pl.pallas_call(
        functools.partial(_norm_proj_kernel, with_forget=with_forget),
        grid=(t // tm, n // tn),
        in_specs=in_specs,
        out_specs=out_specs,
        out_shape=out_shape,
        scratch_shapes=[pltpu.VMEM((tm, d), BF16)],
        compiler_params=_params(("parallel", "arbitrary"), 56),
        name="norm_proj_forget" if with_forget else "norm_proj",
    )(*args)
    return out if with_forget else out[0]


def _rotate(x, cos, sin_signed):
    return x * cos + pltpu.roll(x, HEAD_DIM // 2, 1) * sin_signed


def _retention_kernel(lg_ref, q_ref, k_ref, v_ref, gate_ref, cos_ref, sin_ref, gn_ref, o_ref):
    seq = q_ref.shape[0]
    rows = RET_ROWS
    lg = lg_ref[pl.program_id(1)]

    row = lax.broadcasted_iota(jnp.int32, (rows, HEAD_DIM), 0).astype(F32)
    xi = jnp.exp(lg * (row + 1.0))
    zeta = jnp.exp(lg * (rows - 1.0 - row))
    ti = lax.broadcasted_iota(jnp.int32, (rows, rows), 0)
    si = lax.broadcasted_iota(jnp.int32, (rows, rows), 1)
    visible = (si >> CHUNK_SHIFT) <= (ti >> CHUNK_SHIFT)
    decay = jnp.where(visible, jnp.exp(lg * jnp.abs(ti - si).astype(F32)), 0.0)
    block_decay = jnp.exp(jnp.full((1, HEAD_DIM), lg * rows, F32))
    gn = gn_ref[...]

    state = jnp.zeros((HEAD_DIM, HEAD_DIM), F32)
    for r in range(seq // rows):
        sl = slice(r * rows, (r + 1) * rows)
        cos = cos_ref[sl, :]
        sin = sin_ref[sl, :]
        q = _rotate(q_ref[sl, :].astype(F32), cos, sin) * (HEAD_DIM ** -0.5)
        k = _rotate(k_ref[sl, :].astype(F32), cos, sin)
        v = v_ref[sl, :]
        scores = _dot_nt(q.astype(BF16), k.astype(BF16)) * decay
        out = _dot(scores.astype(BF16), v) + _dot((q * xi).astype(BF16), state.astype(BF16))
        state = state * block_decay + _dot_tn((k * zeta).astype(BF16), v)

        ms = jnp.mean(out * out, axis=-1, keepdims=True)
        y = out * lax.rsqrt(ms + GN_EPS) * gn
        gate = gate_ref[sl, :].astype(F32)
        o_ref[sl, :] = (jax.nn.silu(gate) * y).astype(o_ref.dtype)


def _retention(z, log_g, cos, sin_signed, gn, batch, seq):
    t = z.shape[0]
    col = lambda off: pl.BlockSpec((seq, HEAD_DIM), lambda b, h: (b, off + h))
    tab = pl.BlockSpec((seq, HEAD_DIM), lambda b, h: (0, 0))
    return pl.pallas_call(
        _retention_kernel,
        grid=(batch, N_HEADS),
        in_specs=[
            pl.BlockSpec(memory_space=pltpu.SMEM),
            col(0), col(N_HEADS), col(2 * N_HEADS), col(3 * N_HEADS),
            tab, tab,
            pl.BlockSpec((1, HEAD_DIM), lambda b, h: (0, h)),
        ],
        out_specs=pl.BlockSpec((seq, HEAD_DIM), lambda b, h: (b, h)),
        out_shape=jax.ShapeDtypeStruct((t, N_HEADS * HEAD_DIM), BF16),
        compiler_params=_params(("parallel", "parallel"), 32),
        name="retention",
    )(log_g, z, z, z, z, cos, sin_signed, gn)


def _forget_bias_kernel(ff_ref, b_ref, tri_ref, o_ref):
    seq = ff_ref.shape[0]
    rows = tri_ref.shape[0]
    lane = lax.broadcasted_iota(jnp.int32, (rows, LANES), 1)
    carry = jnp.zeros((1, LANES), F32)
    for blk in range(seq // rows):
        sl = slice(blk * rows, (blk + 1) * rows)
        log_f = jax.nn.log_sigmoid(ff_ref[sl, :] + b_ref[...])
        c = jnp.dot(tri_ref[...], log_f, precision=lax.Precision.HIGHEST,
                    preferred_element_type=F32) + carry
        carry = c[rows - 1:rows, :]
        for h in range(N_HEADS):
            x = jnp.broadcast_to(c[:, h:h + 1], (rows, LANES)) * (-LOG2E)
            hi = x.astype(BF16).astype(F32)
            mid = (x - hi).astype(BF16).astype(F32)
            lo = x - hi - mid
            pieces = jnp.where(lane == 0, hi, jnp.where(lane == 1, mid,
                                                       jnp.where(lane == 2, lo, 0.0)))
            o_ref[h, sl, :] = pieces.astype(o_ref.dtype)


def _forget_bias(ff, b_f, batch, seq):
    rows = CUMSUM_ROWS
    tri = (jnp.arange(rows)[None, :] <= jnp.arange(rows)[:, None]).astype(F32)
    out = pl.pallas_call(
        _forget_bias_kernel,
        grid=(batch,),
        in_specs=[
            pl.BlockSpec((seq, LANES), lambda b: (b, 0)),
            pl.BlockSpec((1, LANES), lambda b: (0, 0)),
            pl.BlockSpec((rows, rows), lambda b: (0, 0)),
        ],
        out_specs=pl.BlockSpec((None, N_HEADS, seq, LANES), lambda b: (b, 0, 0, 0)),
        out_shape=jax.ShapeDtypeStruct((batch, N_HEADS, seq, LANES), BF16),
        compiler_params=_params(("parallel",), 32),
        name="forget_bias",
    )(ff, b_f, tri)
    return out.reshape(batch * N_HEADS * seq, LANES)


def _pipeline_blocks(blocks):
    for task in blocks[0][0]:
        task()
    for n, (_, value_tasks) in enumerate(blocks):
        score_tasks = blocks[n + 1][0] if n + 1 < len(blocks) else []
        done = 0
        for idx, task in enumerate(value_tasks):
            task()
            while done < len(score_tasks) and done * len(value_tasks) < (idx + 1) * len(score_tasks):
                score_tasks[done]()
                done += 1


def _diag_tile_max(s, n_full, band_rows, band_mask):
    parts = []
    for j in range(s.shape[1] // LANES):
        lanes = slice(j * LANES, (j + 1) * LANES)
        nf = n_full(j)
        band = jnp.where(band_mask, s[nf:nf + band_rows, lanes], -jnp.inf)
        mx = _slab_reduce(band, jnp.max)
        if nf:
            mx = jnp.maximum(mx, _slab_reduce(s[:nf, lanes], jnp.max))
        parts.append(mx)
    return jnp.concatenate(parts, axis=1)


def _diag_tile_exp(s_ref, row0, m, n_full, band_rows, band_mask):
    t = s_ref.shape[1]
    strips, sums = [], []
    for j in range(t // LANES):
        lanes = slice(j * LANES, (j + 1) * LANES)
        nf = n_full(j)
        nv = nf + band_rows
        pj = jnp.exp2(s_ref[row0:row0 + nv, lanes] - m[:, lanes])
        band = jnp.where(band_mask, pj[nf:, :], 0.0)
        total = _slab_reduce(band, jnp.sum)
        pieces = [band.astype(BF16)]
        if nf:
            total = total + _slab_reduce(pj[:nf, :], jnp.sum)
            pieces.insert(0, pj[:nf, :].astype(BF16))
        if nv < t:
            pieces.append(jnp.zeros((t - nv, LANES), BF16))
        strips.append(jnp.concatenate(pieces, axis=0))
        sums.append(total)
    return jnp.concatenate(strips, axis=1), jnp.concatenate(sums, axis=1)


def _fox_kernel(q_ref, k_ref, v_ref, fb_ref, o_ref, vt_ref, s_ref):
    seq = q_ref.shape[0]
    t = ATT_T
    vt_ref[...] = v_ref[...].astype(F32).T.astype(BF16)
    lane = lax.broadcasted_iota(jnp.int32, (t, HEAD_DIM), 1)
    ones3 = jnp.where(lane < 3, 1.0, 0.0).astype(BF16)
    band_mask = (lax.broadcasted_iota(jnp.int32, (LANES, LANES), 0)
                 <= lax.broadcasted_iota(jnp.int32, (LANES, LANES), 1))
    n_full = lambda j: j * LANES
    qscale = (HEAD_DIM ** -0.5) * LOG2E

    def query_block(i):
        rows_i = slice(i * t, (i + 1) * t)
        sbuf = s_ref.at[i % 2]
        st = {"m_run": None, "l_run": jnp.zeros((SUBLANES, t), F32),
              "acc": jnp.zeros((HEAD_DIM, t), F32)}

        def score_task(c):
            rows_c = slice(c * t, (c + 1) * t)
            if c == 0:
                qs = (q_ref[rows_i, :].astype(F32) * qscale).astype(BF16)
                st["qa"] = jnp.concatenate([qs, ones3], axis=1)
            ka = jnp.concatenate([k_ref[rows_c, :], fb_ref[rows_c, :]], axis=1)
            s = _dot_nt(ka, st["qa"])
            sbuf[rows_c, :] = s
            if c == i:
                tile_max = _diag_tile_max(s, n_full, LANES, band_mask)
            else:
                tile_max = _slab_reduce(s, jnp.max)
            st["m_run"] = tile_max if c == 0 else jnp.maximum(st["m_run"], tile_max)

        def value_task(c):
            rows_c = slice(c * t, (c + 1) * t)
            if c == 0:
                st["m"] = jnp.max(st["m_run"], axis=0, keepdims=True)
            if c == i:
                p, sums = _diag_tile_exp(sbuf, c * t, st["m"], n_full, LANES, band_mask)
            else:
                pf = jnp.exp2(sbuf[rows_c, :] - st["m"])
                p, sums = pf.astype(BF16), _slab_reduce(pf, jnp.sum)
            st["l_run"] = st["l_run"] + sums
            st["acc"] = st["acc"] + _dot(vt_ref[:, rows_c], p)
            if c == i:
                l = jnp.sum(st["l_run"], axis=0, keepdims=True)
                o_ref[rows_i, :] = (st["acc"] / l).T.astype(o_ref.dtype)

        tiles = range(i + 1)
        return ([functools.partial(score_task, c) for c in tiles],
                [functools.partial(value_task, c) for c in tiles])

    _pipeline_blocks([query_block(i) for i in range(seq // t)])


def _fox_attention(z, fbias, batch, seq):
    t = z.shape[0]
    col = lambda off: pl.BlockSpec((seq, HEAD_DIM), lambda b, h: (b, off + h))
    return pl.pallas_call(
        _fox_kernel,
        grid=(batch, N_HEADS),
        in_specs=[
            col(4 * N_HEADS), col(5 * N_HEADS), col(6 * N_HEADS),
            pl.BlockSpec((seq, LANES), lambda b, h: (b * N_HEADS + h, 0)),
        ],
        out_specs=pl.BlockSpec((seq, HEAD_DIM), lambda b, h: (b, h)),
        out_shape=jax.ShapeDtypeStruct((t, N_HEADS * HEAD_DIM), BF16),
        scratch_shapes=[pltpu.VMEM((HEAD_DIM, seq), BF16),
                        pltpu.VMEM((2, seq, ATT_T), F32)],
        compiler_params=_params(("parallel", "parallel"), 40),
        name="fox_attention",
    )(z, z, z, fbias)


def _diff_kernel(q_ref, k_ref, v_ref, lq1_ref, lk1_ref, lq2_ref, lk2_ref, g_ref, o_ref,
                 vt_ref, s1_ref, s2_ref, p1_ref, p2_ref, acc_ref, *, lambda_init):
    seq = q_ref.shape[0]
    t = ATT_T
    vt_ref[...] = v_ref[...].astype(F32).T.astype(BF16)
    lam = (jnp.exp(jnp.sum(lq1_ref[...] * lk1_ref[...], axis=-1, keepdims=True))
           - jnp.exp(jnp.sum(lq2_ref[...] * lk2_ref[...], axis=-1, keepdims=True))
           + lambda_init)
    band_mask = lax.broadcasted_iota(jnp.int32, (CHUNK, LANES), 1) >= CHUNK
    n_full = lambda j: j * LANES + CHUNK
    qscale = (HEAD_DIM ** -0.5) * LOG2E
    col_slices = (slice(0, HEAD_DIM), slice(HEAD_DIM, 2 * HEAD_DIM))
    p_refs = (p1_ref, p2_ref)

    def query_block(i):
        rows_i = slice(i * t, (i + 1) * t)
        s_refs = (s1_ref.at[i % 2], s2_ref.at[i % 2])
        zero = jnp.zeros((SUBLANES, t), F32)
        st = {"m_run": [None, None], "m": [None, None], "l_run": [zero, zero]}

        def score_task(mi, c):
            rows_c = slice(c * t, (c + 1) * t)
            if mi == 0 and c == 0:
                st["qs"] = (q_ref[rows_i, :].astype(F32) * qscale).astype(BF16)
            cols = col_slices[mi]
            s = _dot_nt(k_ref[rows_c, cols], st["qs"][:, cols])
            s_refs[mi][rows_c, :] = s
            if c == i:
                tile_max = _diag_tile_max(s, n_full, CHUNK, band_mask)
            else:
                tile_max = _slab_reduce(s, jnp.max)
            st["m_run"][mi] = tile_max if c == 0 else jnp.maximum(st["m_run"][mi], tile_max)

        def exp_task(mi, c):
            rows_c = slice(c * t, (c + 1) * t)
            if c == 0:
                st["m"][mi] = jnp.max(st["m_run"][mi], axis=0, keepdims=True)
            if c == i:
                p, sums = _diag_tile_exp(s_refs[mi], c * t, st["m"][mi], n_full, CHUNK,
                                         band_mask)
            else:
                pf = jnp.exp2(s_refs[mi][rows_c, :] - st["m"][mi])
                p, sums = pf.astype(BF16), _slab_reduce(pf, jnp.sum)
            p_refs[mi][rows_c, :] = p
            st["l_run"][mi] = st["l_run"][mi] + sums

        def value_task(c):
            rows_c = slice(c * t, (c + 1) * t)
            if c == 0:
                st["l1"] = jnp.sum(st["l_run"][0], axis=0, keepdims=True)
                l2 = jnp.sum(st["l_run"][1], axis=0, keepdims=True)
                st["rho"] = (lam * st["l1"] / l2).astype(BF16)
            a = p1_ref[rows_c, :] - p2_ref[rows_c, :] * st["rho"]
            av = _dot(vt_ref[:, rows_c], a)
            acc_ref[...] = av if c == 0 else acc_ref[...] + av
            if c == i:
                o = (acc_ref[...] / st["l1"]).T
                ms = jnp.mean(o * o, axis=-1, keepdims=True)
                y = o * lax.rsqrt(ms + GN_EPS) * g_ref[...] * (1.0 - lambda_init)
                o_ref[rows_i, :] = y.astype(o_ref.dtype)

        tiles = range(i + 1)
        scores = [functools.partial(score_task, mi, c) for mi in (0, 1) for c in tiles]
        values = ([functools.partial(exp_task, mi, c) for mi in (0, 1) for c in tiles]
                  + [functools.partial(value_task, c) for c in tiles])
        return scores, values

    _pipeline_blocks([query_block(i) for i in range(seq // t)])


def _diff_attention(z, lq1, lk1, lq2, lk2, subln_g, lambda_init, batch, seq):
    t = z.shape[0]
    width = 2 * HEAD_DIM
    col = lambda off: pl.BlockSpec((seq, width), lambda b, h: (b, off + h))
    vec = pl.BlockSpec((1, HEAD_DIM), lambda b, h: (0, 0))
    return pl.pallas_call(
        functools.partial(_diff_kernel, lambda_init=lambda_init),
        grid=(batch, N_HEADS),
        in_specs=[
            col(0), col(N_HEADS), col(2 * N_HEADS),
            vec, vec, vec, vec,
            pl.BlockSpec((1, width), lambda b, h: (0, 0)),
        ],
        out_specs=pl.BlockSpec((seq, width), lambda b, h: (b, h)),
        out_shape=jax.ShapeDtypeStruct((t, N_HEADS * width), BF16),
        scratch_shapes=[pltpu.VMEM((width, seq), BF16),
                        pltpu.VMEM((2, seq, ATT_T), F32),
                        pltpu.VMEM((2, seq, ATT_T), F32),
                        pltpu.VMEM((seq, ATT_T), BF16),
                        pltpu.VMEM((seq, ATT_T), BF16),
                        pltpu.VMEM((width, ATT_T), F32)],
        compiler_params=_params(("parallel", "parallel"), 48),
        name="diff_attention",
    )(z, z, z, lq1, lk1, lq2, lk2, subln_g)


def _proj_residual_kernel(*refs, n_parts):
    a_refs = refs[:n_parts]
    w_refs = refs[n_parts:2 * n_parts]
    res_ref, o_ref = refs[2 * n_parts:]
    acc = res_ref[...]
    for a_ref, w_ref in zip(a_refs, w_refs):
        acc = acc + _dot(a_ref[...], w_ref[...])
    o_ref[...] = acc


def _proj_residual(parts, w, layer, res):
    t, n = res.shape
    tm = OUT_TM
    kp = parts[0].shape[1]
    n_parts = len(parts)
    a_specs = [pl.BlockSpec((tm, kp), lambda i: (i, 0)) for _ in parts]
    w_specs = [pl.BlockSpec((None, kp, n), functools.partial(lambda i, p: (layer, p, 0), p=p))
               for p in range(n_parts)]
    return pl.pallas_call(
        functools.partial(_proj_residual_kernel, n_parts=n_parts),
        grid=(t // tm,),
        in_specs=a_specs + w_specs + [pl.BlockSpec((tm, n), lambda i: (i, 0))],
        out_specs=pl.BlockSpec((tm, n), lambda i: (i, 0)),
        out_shape=jax.ShapeDtypeStruct((t, n), F32),
        compiler_params=_params(("parallel",), 48),
        name="proj_residual",
    )(*parts, *([w] * n_parts), res)


def _mlp_kernel(h_ref, g_ref, w1_ref, w2_ref, *rest, with_final):
    if with_final:
        fg_ref, o_ref, xn_ref, acc_ref = rest
    else:
        o_ref, xn_ref, acc_ref = rest
    f = pl.program_id(1)

    @pl.when(f == 0)
    def _():
        xn_ref[...] = _rms_normalize(h_ref[...], g_ref[...]).astype(BF16)
        acc_ref[...] = jnp.zeros_like(acc_ref)

    a = jnp.maximum(_dot(xn_ref[...], w1_ref[...]), 0.0)
    acc_ref[...] += _dot((a * a).astype(BF16), w2_ref[...])

    @pl.when(f == pl.num_programs(1) - 1)
    def _():
        out = h_ref[...] + acc_ref[...]
        if with_final:
            out = _rms_normalize(out, fg_ref[...])
        o_ref[...] = out


def _mlp(h, g, w1, w2, layer, final_g=None):
    t, d = h.shape
    dff = w1.shape[2]
    tm, tf = MLP_TM, MLP_TF
    with_final = final_g is not None
    in_specs = [
        pl.BlockSpec((tm, d), lambda i, f: (i, 0)),
        pl.BlockSpec((1, d), lambda i, f: (0, 0)),
        pl.BlockSpec((None, d, tf), lambda i, f: (layer, 0, f)),
        pl.BlockSpec((None, tf, d), lambda i, f: (layer, f, 0)),
    ]
    args = [h, g, w1, w2]
    if with_final:
        in_specs.append(pl.BlockSpec((1, d), lambda i, f: (0, 0)))
        args.append(final_g)
    return pl.pallas_call(
        functools.partial(_mlp_kernel, with_final=with_final),
        grid=(t // tm, dff // tf),
        in_specs=in_specs,
        out_specs=pl.BlockSpec((tm, d), lambda i, f: (i, 0)),
        out_shape=jax.ShapeDtypeStruct((t, d), F32),
        scratch_shapes=[pltpu.VMEM((tm, d), BF16), pltpu.VMEM((tm, d), F32)],
        compiler_params=_params(("parallel", "arbitrary"), 56),
        name="mlp_final" if with_final else "mlp",
    )(*args)


def kernel(x, norm_mix_g, norm_mlp_g, even_w_in, even_b_f, even_ret_gn, even_w_out,
           odd_w_in, odd_lambda_q1, odd_lambda_k1, odd_lambda_q2, odd_lambda_k2,
           odd_subln_g, odd_w_out, mlp_w1, mlp_w2, final_g):
    batch, seq, d = x.shape
    t = batch * seq
    ret_w = N_HEADS * HEAD_DIM
    row = lambda v: v.reshape(1, -1).astype(F32)

    half = HEAD_DIM // 2
    inv = ROPE_BASE ** (-jnp.arange(half, dtype=F32) / half)
    ang = jnp.arange(seq, dtype=F32)[:, None] * inv[None, :]
    cos = jnp.concatenate([jnp.cos(ang), jnp.cos(ang)], axis=-1)
    sin_signed = jnp.concatenate([-jnp.sin(ang), jnp.sin(ang)], axis=-1)
    log_g = jnp.log1p(-(2.0 ** (-5.0 - jnp.arange(N_HEADS, dtype=F32))))

    h = x.reshape(t, d)

    w1 = mlp_w1.astype(BF16)
    w2 = mlp_w2.astype(BF16)
    n_main = 7 * ret_w
    wf = jnp.pad(even_w_in[0, :, n_main:], ((0, 0), (0, LANES - N_HEADS))).astype(BF16)
    z, ff = _norm_proj(h, row(norm_mix_g[0]), even_w_in[:, :, :n_main].astype(BF16), 0,
                       n_main, wf)
    b_f = jnp.pad(even_b_f[0].astype(F32), (0, LANES - N_HEADS)).reshape(1, LANES)
    fbias = _forget_bias(ff, b_f, batch, seq)
    ret = _retention(z, log_g, cos, sin_signed, row(even_ret_gn[0]), batch, seq)
    fox = _fox_attention(z, fbias, batch, seq)
    h = _proj_residual([ret, fox], even_w_out.astype(BF16), 0, h)
    h = _mlp(h, row(norm_mlp_g[0]), w1, w2, 0)

    lambda_init = 0.8 - 0.6 * math.exp(-0.3 * 1)
    z = _norm_proj(h, row(norm_mix_g[1]), odd_w_in.astype(BF16), 0, odd_w_in.shape[2])
    att = _diff_attention(z, row(odd_lambda_q1[0]), row(odd_lambda_k1[0]),
                          row(odd_lambda_q2[0]), row(odd_lambda_k2[0]),
                          row(odd_subln_g[0]), lambda_init, batch, seq)
    h = _proj_residual([att], odd_w_out.astype(BF16), 0, h)
    h = _mlp(h, row(norm_mlp_g[1]), w1, w2, 1, row(final_g))
    return h.reshape(batch, seq, d)
```

```python
import functools
import math

import jax
import jax.numpy as jnp
from jax import lax
from jax.experimental import pallas as pl
from jax.experimental.pallas import tpu as pltpu

F32 = jnp.float32
BF16 = jnp.bfloat16

CHUNK = 64
CHUNK_SHIFT = 6
HEAD_DIM = 128
N_HEADS = 8
ROPE_BASE = 10000.0
RMS_EPS = 1e-6
GN_EPS = 1e-5
LOG2E = 1.4426950408889634
LANES = 128
SUBLANES = 8
MIB = 1024 * 1024

MXU_COLS = 256
PROJ_TM = 1024
PROJ_TN_MAX = 2048
OUT_TM = 512
MLP_TM = 512
MLP_TF = 1024
RET_ROWS = 256
ATT_T = 512
CUMSUM_ROWS = 256


def _params(semantics, vmem_mib):
    return pltpu.CompilerParams(dimension_semantics=semantics,
                                vmem_limit_bytes=vmem_mib * MIB)


def _dot(a, b):
    return jnp.dot(a, b, preferred_element_type=F32)


def _dot_nt(a, b):
    return lax.dot_general(a, b, (((1,), (1,)), ((), ())), preferred_element_type=F32)


def _dot_tn(a, b):
    return lax.dot_general(a, b, (((0,), (0,)), ((), ())), preferred_element_type=F32)


def _rms_normalize(x, g):
    ms = jnp.mean(x * x, axis=-1, keepdims=True)
    return (x * lax.rsqrt(ms + RMS_EPS)) * g


def _slab_reduce(x, op):
    rows, n = x.shape
    return op(x.reshape(rows // SUBLANES, SUBLANES, n), axis=0)


def _norm_proj_kernel(x_ref, g_ref, w_ref, *rest, with_forget, w_transposed):
    if with_forget:
        wf_ref, o_ref, f_ref, xn_ref = rest
    else:
        o_ref, xn_ref = rest

    @pl.when(pl.program_id(1) == 0)
    def _():
        xn_ref[...] = _rms_normalize(x_ref[...], g_ref[...]).astype(BF16)
        if with_forget:
            f_ref[...] = _dot(xn_ref[...], wf_ref[...].astype(BF16))

    matmul = _dot_nt if w_transposed else _dot
    o_ref[...] = matmul(xn_ref[...], w_ref[...]).astype(o_ref.dtype)


def _norm_proj(x, g, w, layer, n, wf=None, w_transposed=False):
    t, d = x.shape
    tm = PROJ_TM
    tn = max(c for c in range(MXU_COLS, PROJ_TN_MAX + 1, MXU_COLS) if n % c == 0)
    with_forget = wf is not None
    if w_transposed:
        w_spec = pl.BlockSpec((None, tn, d), lambda i, j: (layer, j, 0))
    else:
        w_spec = pl.BlockSpec((None, d, tn), lambda i, j: (layer, 0, j))
    in_specs = [
        pl.BlockSpec((tm, d), lambda i, j: (i, 0)),
        pl.BlockSpec((1, d), lambda i, j: (0, 0)),
        w_spec,
    ]
    out_shape = [jax.ShapeDtypeStruct((t, n), BF16)]
    out_specs = [pl.BlockSpec((tm, tn), lambda i, j: (i, j))]
    args = [x, g, w]
    if with_forget:
        in_specs.append(pl.BlockSpec((d, LANES), lambda i, j: (0, 0)))
        out_shape.append(jax.ShapeDtypeStruct((t, LANES), F32))
        out_specs.append(pl.BlockSpec((tm, LANES), lambda i, j: (i, 0)))
        args.append(wf)
    out = pl.pallas_call(
        functools.partial(_norm_proj_kernel, with_forget=with_forget,
                          w_transposed=w_transposed),
        grid=(t // tm, n // tn),
        in_specs=in_specs,
        out_specs=out_specs,
        out_shape=out_shape,
        scratch_shapes=[pltpu.VMEM((tm, d), BF16)],
        compiler_params=_params(("parallel", "arbitrary"), 56),
        name="norm_proj_forget" if with_forget else "norm_proj",
    )(*args)
    return out if with_forget else out[0]


def _rotate(x, cos, sin_signed):
    return x * cos + pltpu.roll(x, HEAD_DIM // 2, 1) * sin_signed


def _retention_kernel(lg_ref, q_ref, k_ref, v_ref, gate_ref, cos_ref, sin_ref, gn_ref, o_ref):
    seq = q_ref.shape[0]
    rows = RET_ROWS
    lg = lg_ref[pl.program_id(1)]

    row = lax.broadcasted_iota(jnp.int32, (rows, HEAD_DIM), 0).astype(F32)
    xi = jnp.exp(lg * (row + 1.0))
    zeta = jnp.exp(lg * (rows - 1.0 - row))
    ti = lax.broadcasted_iota(jnp.int32, (rows, rows), 0)
    si = lax.broadcasted_iota(jnp.int32, (rows, rows), 1)
    visible = (si >> CHUNK_SHIFT) <= (ti >> CHUNK_SHIFT)
    decay = jnp.where(visible, jnp.exp(lg * jnp.abs(ti - si).astype(F32)), 0.0)
    block_decay = jnp.exp(jnp.full((1, HEAD_DIM), lg * rows, F32))
    gn = gn_ref[...]

    state = jnp.zeros((HEAD_DIM, HEAD_DIM), F32)
    for r in range(seq // rows):
        sl = slice(r * rows, (r + 1) * rows)
        cos = cos_ref[sl, :]
        sin = sin_ref[sl, :]
        q = _rotate(q_ref[sl, :].astype(F32), cos, sin) * (HEAD_DIM ** -0.5)
        k = _rotate(k_ref[sl, :].astype(F32), cos, sin)
        v = v_ref[sl, :]
        scores = _dot_nt(q.astype(BF16), k.astype(BF16)) * decay
        out = _dot(scores.astype(BF16), v) + _dot((q * xi).astype(BF16), state.astype(BF16))
        state = state * block_decay + _dot_tn((k * zeta).astype(BF16), v)

        ms = jnp.mean(out * out, axis=-1, keepdims=True)
        y = out * lax.rsqrt(ms + GN_EPS) * gn
        gate = gate_ref[sl, :].astype(F32)
        o_ref[sl, :] = (jax.nn.silu(gate) * y).astype(o_ref.dtype)


def _retention(z, log_g, cos, sin_signed, gn, batch, seq):
    t = z.shape[0]
    col = lambda off: pl.BlockSpec((seq, HEAD_DIM), lambda b, h: (b, off + h))
    tab = pl.BlockSpec((seq, HEAD_DIM), lambda b, h: (0, 0))
    return pl.pallas_call(
        _retention_kernel,
        grid=(batch, N_HEADS),
        in_specs=[
            pl.BlockSpec(memory_space=pltpu.SMEM),
            col(0), col(N_HEADS), col(2 * N_HEADS), col(3 * N_HEADS),
            tab, tab,
            pl.BlockSpec((1, HEAD_DIM), lambda b, h: (0, h)),
        ],
        out_specs=pl.BlockSpec((seq, HEAD_DIM), lambda b, h: (b, h)),
        out_shape=jax.ShapeDtypeStruct((t, N_HEADS * HEAD_DIM), BF16),
        compiler_params=_params(("parallel", "parallel"), 32),
        name="retention",
    )(log_g, z, z, z, z, cos, sin_signed, gn)


def _forget_bias_kernel(ff_ref, b_ref, tri_ref, o_ref):
    seq = ff_ref.shape[0]
    rows = tri_ref.shape[0]
    lane = lax.broadcasted_iota(jnp.int32, (rows, LANES), 1)
    carry = jnp.zeros((1, LANES), F32)
    for blk in range(seq // rows):
        sl = slice(blk * rows, (blk + 1) * rows)
        log_f = jax.nn.log_sigmoid(ff_ref[sl, :] + b_ref[...])
        c = jnp.dot(tri_ref[...], log_f, precision=lax.Precision.HIGHEST,
                    preferred_element_type=F32) + carry
        carry = c[rows - 1:rows, :]
        for h in range(N_HEADS):
            x = jnp.broadcast_to(c[:, h:h + 1], (rows, LANES)) * (-LOG2E)
            hi = x.astype(BF16).astype(F32)
            mid = (x - hi).astype(BF16).astype(F32)
            lo = x - hi - mid
            pieces = jnp.where(lane == 0, hi, jnp.where(lane == 1, mid,
                                                       jnp.where(lane == 2, lo, 0.0)))
            o_ref[h, sl, :] = pieces.astype(o_ref.dtype)


def _forget_bias(ff, b_f, batch, seq):
    rows = CUMSUM_ROWS
    tri = (jnp.arange(rows)[None, :] <= jnp.arange(rows)[:, None]).astype(F32)
    out = pl.pallas_call(
        _forget_bias_kernel,
        grid=(batch,),
        in_specs=[
            pl.BlockSpec((seq, LANES), lambda b: (b, 0)),
            pl.BlockSpec((1, LANES), lambda b: (0, 0)),
            pl.BlockSpec((rows, rows), lambda b: (0, 0)),
        ],
        out_specs=pl.BlockSpec((None, N_HEADS, seq, LANES), lambda b: (b, 0, 0, 0)),
        out_shape=jax.ShapeDtypeStruct((batch, N_HEADS, seq, LANES), BF16),
        compiler_params=_params(("parallel",), 32),
        name="forget_bias",
    )(ff, b_f, tri)
    return out.reshape(batch * N_HEADS * seq, LANES)


def _pipeline_blocks(blocks):
    for task in blocks[0][0]:
        task()
    for n, (_, value_tasks) in enumerate(blocks):
        score_tasks = blocks[n + 1][0] if n + 1 < len(blocks) else []
        done = 0
        for idx, task in enumerate(value_tasks):
            task()
            while done < len(score_tasks) and done * len(value_tasks) < (idx + 1) * len(score_tasks):
                score_tasks[done]()
                done += 1


def _diag_tile_max(s, n_full, band_rows, band_mask):
    parts = []
    for j in range(s.shape[1] // LANES):
        lanes = slice(j * LANES, (j + 1) * LANES)
        nf = n_full(j)
        band = jnp.where(band_mask, s[nf:nf + band_rows, lanes], -jnp.inf)
        mx = _slab_reduce(band, jnp.max)
        if nf:
            mx = jnp.maximum(mx, _slab_reduce(s[:nf, lanes], jnp.max))
        parts.append(mx)
    return jnp.concatenate(parts, axis=1)


def _diag_tile_exp(s_ref, row0, m, n_full, band_rows, band_mask):
    t = s_ref.shape[1]
    strips, sums = [], []
    for j in range(t // LANES):
        lanes = slice(j * LANES, (j + 1) * LANES)
        nf = n_full(j)
        nv = nf + band_rows
        pj = jnp.exp2(s_ref[row0:row0 + nv, lanes] - m[:, lanes])
        band = jnp.where(band_mask, pj[nf:, :], 0.0)
        total = _slab_reduce(band, jnp.sum)
        pieces = [band.astype(BF16)]
        if nf:
            total = total + _slab_reduce(pj[:nf, :], jnp.sum)
            pieces.insert(0, pj[:nf, :].astype(BF16))
        if nv < t:
            pieces.append(jnp.zeros((t - nv, LANES), BF16))
        strips.append(jnp.concatenate(pieces, axis=0))
        sums.append(total)
    return jnp.concatenate(strips, axis=1), jnp.concatenate(sums, axis=1)


def _fox_kernel(q_ref, k_ref, v_ref, fb_ref, o_ref, vt_ref, s_ref):
    seq = q_ref.shape[0]
    t = ATT_T
    vt_ref[...] = v_ref[...].astype(F32).T.astype(BF16)
    lane = lax.broadcasted_iota(jnp.int32, (t, HEAD_DIM), 1)
    ones3 = jnp.where(lane < 3, 1.0, 0.0).astype(BF16)
    band_mask = (lax.broadcasted_iota(jnp.int32, (LANES, LANES), 0)
                 <= lax.broadcasted_iota(jnp.int32, (LANES, LANES), 1))
    n_full = lambda j: j * LANES
    qscale = (HEAD_DIM ** -0.5) * LOG2E

    def query_block(i):
        rows_i = slice(i * t, (i + 1) * t)
        sbuf = s_ref.at[i % 2]
        st = {"m_run": None, "l_run": jnp.zeros((SUBLANES, t), F32),
              "acc": jnp.zeros((HEAD_DIM, t), F32)}

        def score_task(c):
            rows_c = slice(c * t, (c + 1) * t)
            if c == 0:
                qs = (q_ref[rows_i, :].astype(F32) * qscale).astype(BF16)
                st["qa"] = jnp.concatenate([qs, ones3], axis=1)
            ka = jnp.concatenate([k_ref[rows_c, :], fb_ref[rows_c, :]], axis=1)
            s = _dot_nt(ka, st["qa"])
            sbuf[rows_c, :] = s
            if c == i:
                tile_max = _diag_tile_max(s, n_full, LANES, band_mask)
            else:
                tile_max = _slab_reduce(s, jnp.max)
            st["m_run"] = tile_max if c == 0 else jnp.maximum(st["m_run"], tile_max)

        def value_task(c):
            rows_c = slice(c * t, (c + 1) * t)
            if c == 0:
                st["m"] = jnp.max(st["m_run"], axis=0, keepdims=True)
            if c == i:
                p, sums = _diag_tile_exp(sbuf, c * t, st["m"], n_full, LANES, band_mask)
            else:
                pf = jnp.exp2(sbuf[rows_c, :] - st["m"])
                p, sums = pf.astype(BF16), _slab_reduce(pf, jnp.sum)
            st["l_run"] = st["l_run"] + sums
            st["acc"] = st["acc"] + _dot(vt_ref[:, rows_c], p)
            if c == i:
                l = jnp.sum(st["l_run"], axis=0, keepdims=True)
                o_ref[rows_i, :] = (st["acc"] / l).T.astype(o_ref.dtype)

        tiles = range(i + 1)
        return ([functools.partial(score_task, c) for c in tiles],
                [functools.partial(value_task, c) for c in tiles])

    _pipeline_blocks([query_block(i) for i in range(seq // t)])


def _fox_attention(z, fbias, batch, seq):
    t = z.shape[0]
    col = lambda off: pl.BlockSpec((seq, HEAD_DIM), lambda b, h: (b, off + h))
    return pl.pallas_call(
        _fox_kernel,
        grid=(batch, N_HEADS),
        in_specs=[
            col(4 * N_HEADS), col(5 * N_HEADS), col(6 * N_HEADS),
            pl.BlockSpec((seq, LANES), lambda b, h: (b * N_HEADS + h, 0)),
        ],
        out_specs=pl.BlockSpec((seq, HEAD_DIM), lambda b, h: (b, h)),
        out_shape=jax.ShapeDtypeStruct((t, N_HEADS * HEAD_DIM), BF16),
        scratch_shapes=[pltpu.VMEM((HEAD_DIM, seq), BF16),
                        pltpu.VMEM((2, seq, ATT_T), F32)],
        compiler_params=_params(("parallel", "parallel"), 40),
        name="fox_attention",
    )(z, z, z, fbias)


def _diff_kernel(q_ref, k_ref, v_ref, lq1_ref, lk1_ref, lq2_ref, lk2_ref, g_ref, o_ref,
                 vt_ref, s1_ref, s2_ref, p1_ref, p2_ref, acc_ref, *, lambda_init):
    seq = q_ref.shape[0]
    t = ATT_T
    vt_ref[...] = v_ref[...].astype(F32).T.astype(BF16)
    lam = (jnp.exp(jnp.sum(lq1_ref[...] * lk1_ref[...], axis=-1, keepdims=True))
           - jnp.exp(jnp.sum(lq2_ref[...] * lk2_ref[...], axis=-1, keepdims=True))
           + lambda_init)
    band_mask = lax.broadcasted_iota(jnp.int32, (CHUNK, LANES), 1) >= CHUNK
    n_full = lambda j: j * LANES + CHUNK
    qscale = (HEAD_DIM ** -0.5) * LOG2E
    col_slices = (slice(0, HEAD_DIM), slice(HEAD_DIM, 2 * HEAD_DIM))
    p_refs = (p1_ref, p2_ref)

    def query_block(i):
        rows_i = slice(i * t, (i + 1) * t)
        s_refs = (s1_ref.at[i % 2], s2_ref.at[i % 2])
        zero = jnp.zeros((SUBLANES, t), F32)
        st = {"m_run": [None, None], "m": [None, None], "l_run": [zero, zero]}

        def score_task(mi, c):
            rows_c = slice(c * t, (c + 1) * t)
            if mi == 0 and c == 0:
                st["qs"] = (q_ref[rows_i, :].astype(F32) * qscale).astype(BF16)
            cols = col_slices[mi]
            s = _dot_nt(k_ref[rows_c, cols], st["qs"][:, cols])
            s_refs[mi][rows_c, :] = s
            if c == i:
                tile_max = _diag_tile_max(s, n_full, CHUNK, band_mask)
            else:
                tile_max = _slab_reduce(s, jnp.max)
            st["m_run"][mi] = tile_max if c == 0 else jnp.maximum(st["m_run"][mi], tile_max)

        def exp_task(mi, c):
            rows_c = slice(c * t, (c + 1) * t)
            if c == 0:
                st["m"][mi] = jnp.max(st["m_run"][mi], axis=0, keepdims=True)
            if c == i:
                p, sums = _diag_tile_exp(s_refs[mi], c * t, st["m"][mi], n_full, CHUNK,
                                         band_mask)
            else:
                pf = jnp.exp2(s_refs[mi][rows_c, :] - st["m"][mi])
                p, sums = pf.astype(BF16), _slab_reduce(pf, jnp.sum)
            p_refs[mi][rows_c, :] = p
            st["l_run"][mi] = st["l_run"][mi] + sums

        def value_task(c):
            rows_c = slice(c * t, (c + 1) * t)
            if c == 0:
                st["l1"] = jnp.sum(st["l_run"][0], axis=0, keepdims=True)
                l2 = jnp.sum(st["l_run"][1], axis=0, keepdims=True)
                st["rho"] = (lam * st["l1"] / l2).astype(BF16)
            a = p1_ref[rows_c, :] - p2_ref[rows_c, :] * st["rho"]
            av = _dot(vt_ref[:, rows_c], a)
            acc_ref[...] = av if c == 0 else acc_ref[...] + av
            if c == i:
                o = (acc_ref[...] / st["l1"]).T
                ms = jnp.mean(o * o, axis=-1, keepdims=True)
                y = o * lax.rsqrt(ms + GN_EPS) * g_ref[...] * (1.0 - lambda_init)
                o_ref[rows_i, :] = y.astype(o_ref.dtype)

        tiles = range(i + 1)
        scores = [functools.partial(score_task, mi, c) for mi in (0, 1) for c in tiles]
        values = ([functools.partial(exp_task, mi, c) for mi in (0, 1) for c in tiles]
                  + [functools.partial(value_task, c) for c in tiles])
        return scores, values

    _pipeline_blocks([query_block(i) for i in range(seq // t)])


def _diff_attention(z, lq1, lk1, lq2, lk2, subln_g, lambda_init, batch, seq):
    t = z.shape[0]
    width = 2 * HEAD_DIM
    col = lambda off: pl.BlockSpec((seq, width), lambda b, h: (b, off + h))
    vec = pl.BlockSpec((1, HEAD_DIM), lambda b, h: (0, 0))
    return pl.pallas_call(
        functools.partial(_diff_kernel, lambda_init=lambda_init),
        grid=(batch, N_HEADS),
        in_specs=[
            col(0), col(N_HEADS), col(2 * N_HEADS),
            vec, vec, vec, vec,
            pl.BlockSpec((1, width), lambda b, h: (0, 0)),
        ],
        out_specs=pl.BlockSpec((seq, width), lambda b, h: (b, h)),
        out_shape=jax.ShapeDtypeStruct((t, N_HEADS * width), BF16),
        scratch_shapes=[pltpu.VMEM((width, seq), BF16),
                        pltpu.VMEM((2, seq, ATT_T), F32),
                        pltpu.VMEM((2, seq, ATT_T), F32),
                        pltpu.VMEM((seq, ATT_T), BF16),
                        pltpu.VMEM((seq, ATT_T), BF16),
                        pltpu.VMEM((width, ATT_T), F32)],
        compiler_params=_params(("parallel", "parallel"), 48),
        name="diff_attention",
    )(z, z, z, lq1, lk1, lq2, lk2, subln_g)


def _proj_residual_kernel(*refs, n_parts):
    a_refs = refs[:n_parts]
    w_refs = refs[n_parts:2 * n_parts]
    res_ref, o_ref = refs[2 * n_parts:]
    acc = res_ref[...]
    for a_ref, w_ref in zip(a_refs, w_refs):
        acc = acc + _dot(a_ref[...], w_ref[...])
    o_ref[...] = acc


def _proj_residual(parts, w, layer, res):
    t, n = res.shape
    tm = OUT_TM
    kp = parts[0].shape[1]
    n_parts = len(parts)
    a_specs = [pl.BlockSpec((tm, kp), lambda i: (i, 0)) for _ in parts]
    w_specs = [pl.BlockSpec((None, kp, n), functools.partial(lambda i, p: (layer, p, 0), p=p))
               for p in range(n_parts)]
    return pl.pallas_call(
        functools.partial(_proj_residual_kernel, n_parts=n_parts),
        grid=(t // tm,),
        in_specs=a_specs + w_specs + [pl.BlockSpec((tm, n), lambda i: (i, 0))],
        out_specs=pl.BlockSpec((tm, n), lambda i: (i, 0)),
        out_shape=jax.ShapeDtypeStruct((t, n), F32),
        compiler_params=_params(("parallel",), 48),
        name="proj_residual",
    )(*parts, *([w] * n_parts), res)


def _mlp_kernel(h_ref, g_ref, w1_ref, w2_ref, *rest, with_final):
    if with_final:
        fg_ref, o_ref, xn_ref, acc_ref = rest
    else:
        o_ref, xn_ref, acc_ref = rest
    f = pl.program_id(1)

    @pl.when(f == 0)
    def _():
        xn_ref[...] = _rms_normalize(h_ref[...], g_ref[...]).astype(BF16)
        acc_ref[...] = jnp.zeros_like(acc_ref)

    a = jnp.maximum(_dot(xn_ref[...], w1_ref[...]), 0.0)
    acc_ref[...] += _dot((a * a).astype(BF16), w2_ref[...])

    @pl.when(f == pl.num_programs(1) - 1)
    def _():
        out = h_ref[...] + acc_ref[...]
        if with_final:
            out = _rms_normalize(out, fg_ref[...])
        o_ref[...] = out


def _mlp(h, g, w1, w2, layer, final_g=None):
    t, d = h.shape
    dff = w1.shape[2]
    tm, tf = MLP_TM, MLP_TF
    with_final = final_g is not None
    in_specs = [
        pl.BlockSpec((tm, d), lambda i, f: (i, 0)),
        pl.BlockSpec((1, d), lambda i, f: (0, 0)),
        pl.BlockSpec((None, d, tf), lambda i, f: (layer, 0, f)),
        pl.BlockSpec((None, tf, d), lambda i, f: (layer, f, 0)),
    ]
    args = [h, g, w1, w2]
    if with_final:
        in_specs.append(pl.BlockSpec((1, d), lambda i, f: (0, 0)))
        args.append(final_g)
    return pl.pallas_call(
        functools.partial(_mlp_kernel, with_final=with_final),
        grid=(t // tm, dff // tf),
        in_specs=in_specs,
        out_specs=pl.BlockSpec((tm, d), lambda i, f: (i, 0)),
        out_shape=jax.ShapeDtypeStruct((t, d), F32),
        scratch_shapes=[pltpu.VMEM((tm, d), BF16), pltpu.VMEM((tm, d), F32)],
        compiler_params=_params(("parallel", "arbitrary"), 56),
        name="mlp_final" if with_final else "mlp",
    )(*args)


def kernel(x, norm_mix_g, norm_mlp_g, even_w_in, even_b_f, even_ret_gn, even_w_out,
           odd_w_in, odd_lambda_q1, odd_lambda_k1, odd_lambda_q2, odd_lambda_k2,
           odd_subln_g, odd_w_out, mlp_w1, mlp_w2, final_g):
    batch, seq, d = x.shape
    t = batch * seq
    ret_w = N_HEADS * HEAD_DIM
    row = lambda v: v.reshape(1, -1).astype(F32)

    half = HEAD_DIM // 2
    inv = ROPE_BASE ** (-jnp.arange(half, dtype=F32) / half)
    ang = jnp.arange(seq, dtype=F32)[:, None] * inv[None, :]
    cos = jnp.concatenate([jnp.cos(ang), jnp.cos(ang)], axis=-1)
    sin_signed = jnp.concatenate([-jnp.sin(ang), jnp.sin(ang)], axis=-1)
    log_g = jnp.log1p(-(2.0 ** (-5.0 - jnp.arange(N_HEADS, dtype=F32))))

    h = x.reshape(t, d)

    w1 = mlp_w1.astype(BF16)
    w2 = mlp_w2.astype(BF16)
    n_main = 7 * ret_w
    wf = jnp.pad(even_w_in[0, :, n_main:], ((0, 0), (0, LANES - N_HEADS)))
    w_in_t = jnp.swapaxes(even_w_in, 1, 2).astype(BF16)
    z, ff = _norm_proj(h, row(norm_mix_g[0]), w_in_t, 0, n_main, wf, w_transposed=True)
    b_f = jnp.pad(even_b_f[0].astype(F32), (0, LANES - N_HEADS)).reshape(1, LANES)
    fbias = _forget_bias(ff, b_f, batch, seq)
    ret = _retention(z, log_g, cos, sin_signed, row(even_ret_gn[0]), batch, seq)
    fox = _fox_attention(z, fbias, batch, seq)
    h = _proj_residual([ret, fox], even_w_out.astype(BF16), 0, h)
    h = _mlp(h, row(norm_mlp_g[0]), w1, w2, 0)

    lambda_init = 0.8 - 0.6 * math.exp(-0.3 * 1)
    z = _norm_proj(h, row(norm_mix_g[1]), odd_w_in.astype(BF16), 0, odd_w_in.shape[2])
    att = _diff_attention(z, row(odd_lambda_q1[0]), row(odd_lambda_k1[0]),
                          row(odd_lambda_q2[0]), row(odd_lambda_k2[0]),
                          row(odd_subln_g[0]), lambda_init, batch, seq)
    h = _proj_residual([att], odd_w_out.astype(BF16), 0, h)
    h = _mlp(h, row(norm_mlp_g[1]), w1, w2, 1, row(final_g))
    return h.reshape(batch, seq, d)
```

```python
import functools
import math

import jax
import jax.numpy as jnp
from jax import lax
from jax.experimental import pallas as pl
from jax.experimental.pallas import tpu as pltpu

F32 = jnp.float32
BF16 = jnp.bfloat16

CHUNK = 64
CHUNK_SHIFT = 6
HEAD_DIM = 128
N_HEADS = 8
ROPE_BASE = 10000.0
RMS_EPS = 1e-6
GN_EPS = 1e-5
LOG2E = 1.4426950408889634
LANES = 128
SUBLANES = 8
BF16_ROWS = 16
MIB = 1024 * 1024

MXU_COLS = 256
PROJ_TM = 1024
PROJ_TN_MAX = 2048
OUT_TM = 512
MLP_TM = 512
MLP_TF = 1024
RET_ROWS = 256
ATT_T = 512
CUMSUM_ROWS = 256


def _params(semantics, vmem_mib):
    return pltpu.CompilerParams(dimension_semantics=semantics,
                                vmem_limit_bytes=vmem_mib * MIB)


def _dot(a, b):
    return jnp.dot(a, b, preferred_element_type=F32)


def _dot_nt(a, b):
    return lax.dot_general(a, b, (((1,), (1,)), ((), ())), preferred_element_type=F32)


def _dot_tn(a, b):
    return lax.dot_general(a, b, (((0,), (0,)), ((), ())), preferred_element_type=F32)


def _rms_normalize(x, g):
    ms = jnp.mean(x * x, axis=-1, keepdims=True)
    return (x * lax.rsqrt(ms + RMS_EPS)) * g


def _slab_reduce(x, op):
    rows, n = x.shape
    return op(x.reshape(rows // SUBLANES, SUBLANES, n), axis=0)


def _cast_rider(arr, layer, n_steps, step_index):
    rows, cols = arr.shape[1:]
    slab = max(BF16_ROWS, rows // n_steps)
    share = slab * n_steps // rows
    in_spec = pl.BlockSpec((None, slab, cols),
                           lambda *ids: (layer, step_index(*ids) // share, 0))
    out_spec = pl.BlockSpec((slab, cols), lambda *ids: (step_index(*ids) // share, 0))
    return in_spec, out_spec, jax.ShapeDtypeStruct((rows, cols), BF16)


def _run_cast_riders(src_refs, dst_refs):
    for src, dst in zip(src_refs, dst_refs):
        dst[...] = src[...].astype(dst.dtype)


def _norm_proj_kernel(*refs, with_forget, w_transposed, n_cast):
    x_ref, g_ref, w_ref = refs[:3]
    refs = list(refs[3:])
    wf_ref = refs.pop(0) if with_forget else None
    cast_src = [refs.pop(0) for _ in range(n_cast)]
    o_ref = refs.pop(0)
    f_ref = refs.pop(0) if with_forget else None
    cast_dst = [refs.pop(0) for _ in range(n_cast)]
    (xn_ref,) = refs

    @pl.when(pl.program_id(1) == 0)
    def _():
        xn_ref[...] = _rms_normalize(x_ref[...], g_ref[...]).astype(BF16)
        if with_forget:
            f_ref[...] = _dot(xn_ref[...], wf_ref[...].astype(BF16))

    _run_cast_riders(cast_src, cast_dst)
    matmul = _dot_nt if w_transposed else _dot
    o_ref[...] = matmul(xn_ref[...], w_ref[...]).astype(o_ref.dtype)


def _norm_proj(x, g, w, layer, n, wf=None, w_transposed=False, cast=()):
    t, d = x.shape
    tm = PROJ_TM
    tn = max(c for c in range(MXU_COLS, PROJ_TN_MAX + 1, MXU_COLS) if n % c == 0)
    with_forget = wf is not None
    n_col_steps = n // tn
    riders = [_cast_rider(arr, lyr, (t // tm) * n_col_steps, lambda i, j: i * n_col_steps + j)
              for arr, lyr in cast]
    if w_transposed:
        w_spec = pl.BlockSpec((None, tn, d), lambda i, j: (layer, j, 0))
    else:
        w_spec = pl.BlockSpec((None, d, tn), lambda i, j: (layer, 0, j))
    in_specs = [
        pl.BlockSpec((tm, d), lambda i, j: (i, 0)),
        pl.BlockSpec((1, d), lambda i, j: (0, 0)),
        w_spec,
    ]
    out_shape = [jax.ShapeDtypeStruct((t, n), BF16)]
    out_specs = [pl.BlockSpec((tm, tn), lambda i, j: (i, j))]
    args = [x, g, w]
    if with_forget:
        in_specs.append(pl.BlockSpec((d, LANES), lambda i, j: (0, 0)))
        out_shape.append(jax.ShapeDtypeStruct((t, LANES), F32))
        out_specs.append(pl.BlockSpec((tm, LANES), lambda i, j: (i, 0)))
        args.append(wf)
    for (in_spec, out_spec, shape), (arr, _) in zip(riders, cast):
        in_specs.append(in_spec)
        out_specs.append(out_spec)
        out_shape.append(shape)
        args.append(arr)
    out = pl.pallas_call(
        functools.partial(_norm_proj_kernel, with_forget=with_forget,
                          w_transposed=w_transposed, n_cast=len(cast)),
        grid=(t // tm, n_col_steps),
        in_specs=in_specs,
        out_specs=out_specs,
        out_shape=out_shape,
        scratch_shapes=[pltpu.VMEM((tm, d), BF16)],
        compiler_params=_params(("parallel", "arbitrary"), 56),
        name="norm_proj_forget" if with_forget else "norm_proj",
    )(*args)
    return out if len(out) > 1 else out[0]


def _rotate(x, cos, sin_signed):
    return x * cos + pltpu.roll(x, HEAD_DIM // 2, 1) * sin_signed


def _retention_kernel(lg_ref, q_ref, k_ref, v_ref, gate_ref, cos_ref, sin_ref, gn_ref, o_ref):
    seq = q_ref.shape[0]
    rows = RET_ROWS
    lg = lg_ref[pl.program_id(1)]

    row = lax.broadcasted_iota(jnp.int32, (rows, HEAD_DIM), 0).astype(F32)
    xi = jnp.exp(lg * (row + 1.0))
    zeta = jnp.exp(lg * (rows - 1.0 - row))
    ti = lax.broadcasted_iota(jnp.int32, (rows, rows), 0)
    si = lax.broadcasted_iota(jnp.int32, (rows, rows), 1)
    visible = (si >> CHUNK_SHIFT) <= (ti >> CHUNK_SHIFT)
    decay = jnp.where(visible, jnp.exp(lg * jnp.abs(ti - si).astype(F32)), 0.0)
    block_decay = jnp.exp(jnp.full((1, HEAD_DIM), lg * rows, F32))
    gn = gn_ref[...]

    state = jnp.zeros((HEAD_DIM, HEAD_DIM), F32)
    for r in range(seq // rows):
        sl = slice(r * rows, (r + 1) * rows)
        cos = cos_ref[sl, :]
        sin = sin_ref[sl, :]
        q = _rotate(q_ref[sl, :].astype(F32), cos, sin) * (HEAD_DIM ** -0.5)
        k = _rotate(k_ref[sl, :].astype(F32), cos, sin)
        v = v_ref[sl, :]
        scores = _dot_nt(q.astype(BF16), k.astype(BF16)) * decay
        out = _dot(scores.astype(BF16), v) + _dot((q * xi).astype(BF16), state.astype(BF16))
        state = state * block_decay + _dot_tn((k * zeta).astype(BF16), v)

        ms = jnp.mean(out * out, axis=-1, keepdims=True)
        y = out * lax.rsqrt(ms + GN_EPS) * gn
        gate = gate_ref[sl, :].astype(F32)
        o_ref[sl, :] = (jax.nn.silu(gate) * y).astype(o_ref.dtype)


def _retention(z, log_g, cos, sin_signed, gn, batch, seq):
    t = z.shape[0]
    col = lambda off: pl.BlockSpec((seq, HEAD_DIM), lambda b, h: (b, off + h))
    tab = pl.BlockSpec((seq, HEAD_DIM), lambda b, h: (0, 0))
    return pl.pallas_call(
        _retention_kernel,
        grid=(batch, N_HEADS),
        in_specs=[
            pl.BlockSpec(memory_space=pltpu.SMEM),
            col(0), col(N_HEADS), col(2 * N_HEADS), col(3 * N_HEADS),
            tab, tab,
            pl.BlockSpec((1, HEAD_DIM), lambda b, h: (0, h)),
        ],
        out_specs=pl.BlockSpec((seq, HEAD_DIM), lambda b, h: (b, h)),
        out_shape=jax.ShapeDtypeStruct((t, N_HEADS * HEAD_DIM), BF16),
        compiler_params=_params(("parallel", "parallel"), 32),
        name="retention",
    )(log_g, z, z, z, z, cos, sin_signed, gn)


def _forget_bias_kernel(ff_ref, b_ref, tri_ref, o_ref):
    seq = ff_ref.shape[0]
    rows = tri_ref.shape[0]
    lane = lax.broadcasted_iota(jnp.int32, (rows, LANES), 1)
    carry = jnp.zeros((1, LANES), F32)
    for blk in range(seq // rows):
        sl = slice(blk * rows, (blk + 1) * rows)
        log_f = jax.nn.log_sigmoid(ff_ref[sl, :] + b_ref[...])
        c = jnp.dot(tri_ref[...], log_f, precision=lax.Precision.HIGHEST,
                    preferred_element_type=F32) + carry
        carry = c[rows - 1:rows, :]
        for h in range(N_HEADS):
            x = jnp.broadcast_to(c[:, h:h + 1], (rows, LANES)) * (-LOG2E)
            hi = x.astype(BF16).astype(F32)
            mid = (x - hi).astype(BF16).astype(F32)
            lo = x - hi - mid
            pieces = jnp.where(lane == 0, hi, jnp.where(lane == 1, mid,
                                                       jnp.where(lane == 2, lo, 0.0)))
            o_ref[h, sl, :] = pieces.astype(o_ref.dtype)


def _forget_bias(ff, b_f, batch, seq):
    rows = CUMSUM_ROWS
    tri = (jnp.arange(rows)[None, :] <= jnp.arange(rows)[:, None]).astype(F32)
    out = pl.pallas_call(
        _forget_bias_kernel,
        grid=(batch,),
        in_specs=[
            pl.BlockSpec((seq, LANES), lambda b: (b, 0)),
            pl.BlockSpec((1, LANES), lambda b: (0, 0)),
            pl.BlockSpec((rows, rows), lambda b: (0, 0)),
        ],
        out_specs=pl.BlockSpec((None, N_HEADS, seq, LANES), lambda b: (b, 0, 0, 0)),
        out_shape=jax.ShapeDtypeStruct((batch, N_HEADS, seq, LANES), BF16),
        compiler_params=_params(("parallel",), 32),
        name="forget_bias",
    )(ff, b_f, tri)
    return out.reshape(batch * N_HEADS * seq, LANES)


def _pipeline_blocks(blocks):
    for task in blocks[0][0]:
        task()
    for n, (_, value_tasks) in enumerate(blocks):
        score_tasks = blocks[n + 1][0] if n + 1 < len(blocks) else []
        done = 0
        for idx, task in enumerate(value_tasks):
            task()
            while done < len(score_tasks) and done * len(value_tasks) < (idx + 1) * len(score_tasks):
                score_tasks[done]()
                done += 1


def _diag_tile_max(s, n_full, band_rows, band_mask):
    parts = []
    for j in range(s.shape[1] // LANES):
        lanes = slice(j * LANES, (j + 1) * LANES)
        nf = n_full(j)
        band = jnp.where(band_mask, s[nf:nf + band_rows, lanes], -jnp.inf)
        mx = _slab_reduce(band, jnp.max)
        if nf:
            mx = jnp.maximum(mx, _slab_reduce(s[:nf, lanes], jnp.max))
        parts.append(mx)
    return jnp.concatenate(parts, axis=1)


def _diag_tile_exp(s_ref, row0, m, n_full, band_rows, band_mask):
    t = s_ref.shape[1]
    strips, sums = [], []
    for j in range(t // LANES):
        lanes = slice(j * LANES, (j + 1) * LANES)
        nf = n_full(j)
        nv = nf + band_rows
        pj = jnp.exp2(s_ref[row0:row0 + nv, lanes] - m[:, lanes])
        band = jnp.where(band_mask, pj[nf:, :], 0.0)
        total = _slab_reduce(band, jnp.sum)
        pieces = [band.astype(BF16)]
        if nf:
            total = total + _slab_reduce(pj[:nf, :], jnp.sum)
            pieces.insert(0, pj[:nf, :].astype(BF16))
        if nv < t:
            pieces.append(jnp.zeros((t - nv, LANES), BF16))
        strips.append(jnp.concatenate(pieces, axis=0))
        sums.append(total)
    return jnp.concatenate(strips, axis=1), jnp.concatenate(sums, axis=1)


def _fox_kernel(q_ref, k_ref, v_ref, fb_ref, o_ref, vt_ref, s_ref):
    seq = q_ref.shape[0]
    t = ATT_T
    vt_ref[...] = v_ref[...].astype(F32).T.astype(BF16)
    lane = lax.broadcasted_iota(jnp.int32, (t, HEAD_DIM), 1)
    ones3 = jnp.where(lane < 3, 1.0, 0.0).astype(BF16)
    band_mask = (lax.broadcasted_iota(jnp.int32, (LANES, LANES), 0)
                 <= lax.broadcasted_iota(jnp.int32, (LANES, LANES), 1))
    n_full = lambda j: j * LANES
    qscale = (HEAD_DIM ** -0.5) * LOG2E

    def query_block(i):
        rows_i = slice(i * t, (i + 1) * t)
        sbuf = s_ref.at[i % 2]
        st = {"m_run": None, "l_run": jnp.zeros((SUBLANES, t), F32),
              "acc": jnp.zeros((HEAD_DIM, t), F32)}

        def score_task(c):
            rows_c = slice(c * t, (c + 1) * t)
            if c == 0:
                qs = (q_ref[rows_i, :].astype(F32) * qscale).astype(BF16)
                st["qa"] = jnp.concatenate([qs, ones3], axis=1)
            ka = jnp.concatenate([k_ref[rows_c, :], fb_ref[rows_c, :]], axis=1)
            s = _dot_nt(ka, st["qa"])
            sbuf[rows_c, :] = s
            if c == i:
                tile_max = _diag_tile_max(s, n_full, LANES, band_mask)
            else:
                tile_max = _slab_reduce(s, jnp.max)
            st["m_run"] = tile_max if c == 0 else jnp.maximum(st["m_run"], tile_max)

        def value_task(c):
            rows_c = slice(c * t, (c + 1) * t)
            if c == 0:
                st["m"] = jnp.max(st["m_run"], axis=0, keepdims=True)
            if c == i:
                p, sums = _diag_tile_exp(sbuf, c * t, st["m"], n_full, LANES, band_mask)
            else:
                pf = jnp.exp2(sbuf[rows_c, :] - st["m"])
                p, sums = pf.astype(BF16), _slab_reduce(pf, jnp.sum)
            st["l_run"] = st["l_run"] + sums
            st["acc"] = st["acc"] + _dot(vt_ref[:, rows_c], p)
            if c == i:
                l = jnp.sum(st["l_run"], axis=0, keepdims=True)
                o_ref[rows_i, :] = (st["acc"] / l).T.astype(o_ref.dtype)

        tiles = range(i + 1)
        return ([functools.partial(score_task, c) for c in tiles],
                [functools.partial(value_task, c) for c in tiles])

    _pipeline_blocks([query_block(i) for i in range(seq // t)])


def _fox_attention(z, fbias, batch, seq):
    t = z.shape[0]
    col = lambda off: pl.BlockSpec((seq, HEAD_DIM), lambda b, h: (b, off + h))
    return pl.pallas_call(
        _fox_kernel,
        grid=(batch, N_HEADS),
        in_specs=[
            col(4 * N_HEADS), col(5 * N_HEADS), col(6 * N_HEADS),
            pl.BlockSpec((seq, LANES), lambda b, h: (b * N_HEADS + h, 0)),
        ],
        out_specs=pl.BlockSpec((seq, HEAD_DIM), lambda b, h: (b, h)),
        out_shape=jax.ShapeDtypeStruct((t, N_HEADS * HEAD_DIM), BF16),
        scratch_shapes=[pltpu.VMEM((HEAD_DIM, seq), BF16),
                        pltpu.VMEM((2, seq, ATT_T), F32)],
        compiler_params=_params(("parallel", "parallel"), 40),
        name="fox_attention",
    )(z, z, z, fbias)


def _diff_kernel(q_ref, k_ref, v_ref, lq1_ref, lk1_ref, lq2_ref, lk2_ref, g_ref, o_ref,
                 vt_ref, s1_ref, s2_ref, p1_ref, p2_ref, acc_ref, *, lambda_init):
    seq = q_ref.shape[0]
    t = ATT_T
    vt_ref[...] = v_ref[...].astype(F32).T.astype(BF16)
    lam = (jnp.exp(jnp.sum(lq1_ref[...] * lk1_ref[...], axis=-1, keepdims=True))
           - jnp.exp(jnp.sum(lq2_ref[...] * lk2_ref[...], axis=-1, keepdims=True))
           + lambda_init)
    band_mask = lax.broadcasted_iota(jnp.int32, (CHUNK, LANES), 1) >= CHUNK
    n_full = lambda j: j * LANES + CHUNK
    qscale = (HEAD_DIM ** -0.5) * LOG2E
    col_slices = (slice(0, HEAD_DIM), slice(HEAD_DIM, 2 * HEAD_DIM))
    p_refs = (p1_ref, p2_ref)

    def query_block(i):
        rows_i = slice(i * t, (i + 1) * t)
        s_refs = (s1_ref.at[i % 2], s2_ref.at[i % 2])
        zero = jnp.zeros((SUBLANES, t), F32)
        st = {"m_run": [None, None], "m": [None, None], "l_run": [zero, zero]}

        def score_task(mi, c):
            rows_c = slice(c * t, (c + 1) * t)
            if mi == 0 and c == 0:
                st["qs"] = (q_ref[rows_i, :].astype(F32) * qscale).astype(BF16)
            cols = col_slices[mi]
            s = _dot_nt(k_ref[rows_c, cols], st["qs"][:, cols])
            s_refs[mi][rows_c, :] = s
            if c == i:
                tile_max = _diag_tile_max(s, n_full, CHUNK, band_mask)
            else:
                tile_max = _slab_reduce(s, jnp.max)
            st["m_run"][mi] = tile_max if c == 0 else jnp.maximum(st["m_run"][mi], tile_max)

        def exp_task(mi, c):
            rows_c = slice(c * t, (c + 1) * t)
            if c == 0:
                st["m"][mi] = jnp.max(st["m_run"][mi], axis=0, keepdims=True)
            if c == i:
                p, sums = _diag_tile_exp(s_refs[mi], c * t, st["m"][mi], n_full, CHUNK,
                                         band_mask)
            else:
                pf = jnp.exp2(s_refs[mi][rows_c, :] - st["m"][mi])
                p, sums = pf.astype(BF16), _slab_reduce(pf, jnp.sum)
            p_refs[mi][rows_c, :] = p
            st["l_run"][mi] = st["l_run"][mi] + sums

        def value_task(c):
            rows_c = slice(c * t, (c + 1) * t)
            if c == 0:
                st["l1"] = jnp.sum(st["l_run"][0], axis=0, keepdims=True)
                l2 = jnp.sum(st["l_run"][1], axis=0, keepdims=True)
                st["rho"] = (lam * st["l1"] / l2).astype(BF16)
            a = p1_ref[rows_c, :] - p2_ref[rows_c, :] * st["rho"]
            av = _dot(vt_ref[:, rows_c], a)
            acc_ref[...] = av if c == 0 else acc_ref[...] + av
            if c == i:
                o = (acc_ref[...] / st["l1"]).T
                ms = jnp.mean(o * o, axis=-1, keepdims=True)
                y = o * lax.rsqrt(ms + GN_EPS) * g_ref[...] * (1.0 - lambda_init)
                o_ref[rows_i, :] = y.astype(o_ref.dtype)

        tiles = range(i + 1)
        scores = [functools.partial(score_task, mi, c) for mi in (0, 1) for c in tiles]
        values = ([functools.partial(exp_task, mi, c) for mi in (0, 1) for c in tiles]
                  + [functools.partial(value_task, c) for c in tiles])
        return scores, values

    _pipeline_blocks([query_block(i) for i in range(seq // t)])


def _diff_attention(z, lq1, lk1, lq2, lk2, subln_g, lambda_init, batch, seq):
    t = z.shape[0]
    width = 2 * HEAD_DIM
    col = lambda off: pl.BlockSpec((seq, width), lambda b, h: (b, off + h))
    vec = pl.BlockSpec((1, HEAD_DIM), lambda b, h: (0, 0))
    return pl.pallas_call(
        functools.partial(_diff_kernel, lambda_init=lambda_init),
        grid=(batch, N_HEADS),
        in_specs=[
            col(0), col(N_HEADS), col(2 * N_HEADS),
            vec, vec, vec, vec,
            pl.BlockSpec((1, width), lambda b, h: (0, 0)),
        ],
        out_specs=pl.BlockSpec((seq, width), lambda b, h: (b, h)),
        out_shape=jax.ShapeDtypeStruct((t, N_HEADS * width), BF16),
        scratch_shapes=[pltpu.VMEM((width, seq), BF16),
                        pltpu.VMEM((2, seq, ATT_T), F32),
                        pltpu.VMEM((2, seq, ATT_T), F32),
                        pltpu.VMEM((seq, ATT_T), BF16),
                        pltpu.VMEM((seq, ATT_T), BF16),
                        pltpu.VMEM((width, ATT_T), F32)],
        compiler_params=_params(("parallel", "parallel"), 48),
        name="diff_attention",
    )(z, z, z, lq1, lk1, lq2, lk2, subln_g)


def _proj_residual_kernel(*refs, n_parts, n_cast):
    refs = list(refs)
    a_refs = [refs.pop(0) for _ in range(n_parts)]
    w_refs = [refs.pop(0) for _ in range(n_parts)]
    res_ref = refs.pop(0)
    cast_src = [refs.pop(0) for _ in range(n_cast)]
    o_ref = refs.pop(0)
    cast_dst = refs
    _run_cast_riders(cast_src, cast_dst)
    acc = res_ref[...]
    for a_ref, w_ref in zip(a_refs, w_refs):
        acc = acc + _dot(a_ref[...], w_ref[...])
    o_ref[...] = acc


def _proj_residual(parts, w, res, cast=()):
    t, n = res.shape
    tm = OUT_TM
    kp = parts[0].shape[1]
    n_parts = len(parts)
    in_specs = [pl.BlockSpec((tm, kp), lambda i: (i, 0)) for _ in parts]
    in_specs += [pl.BlockSpec((kp, n), functools.partial(lambda i, p: (p, 0), p=p))
                 for p in range(n_parts)]
    in_specs.append(pl.BlockSpec((tm, n), lambda i: (i, 0)))
    args = [*parts, *([w] * n_parts), res]
    out_specs = [pl.BlockSpec((tm, n), lambda i: (i, 0))]
    out_shape = [jax.ShapeDtypeStruct((t, n), F32)]
    for arr, lyr in cast:
        in_spec, out_spec, shape = _cast_rider(arr, lyr, t // tm, lambda i: i)
        in_specs.append(in_spec)
        out_specs.append(out_spec)
        out_shape.append(shape)
        args.append(arr)
    out = pl.pallas_call(
        functools.partial(_proj_residual_kernel, n_parts=n_parts, n_cast=len(cast)),
        grid=(t // tm,),
        in_specs=in_specs,
        out_specs=out_specs,
        out_shape=out_shape,
        compiler_params=_params(("parallel",), 48),
        name="proj_residual",
    )(*args)
    return out if len(out) > 1 else out[0]


def _mlp_kernel(h_ref, g_ref, w1_ref, w2_ref, *refs, with_final, n_cast):
    refs = list(refs)
    fg_ref = refs.pop(0) if with_final else None
    cast_src = [refs.pop(0) for _ in range(n_cast)]
    o_ref = refs.pop(0)
    cast_dst = [refs.pop(0) for _ in range(n_cast)]
    xn_ref, acc_ref = refs
    f = pl.program_id(1)

    @pl.when(f == 0)
    def _():
        xn_ref[...] = _rms_normalize(h_ref[...], g_ref[...]).astype(BF16)
        acc_ref[...] = jnp.zeros_like(acc_ref)

    _run_cast_riders(cast_src, cast_dst)
    a = jnp.maximum(_dot(xn_ref[...], w1_ref[...]), 0.0)
    acc_ref[...] += _dot((a * a).astype(BF16), w2_ref[...])

    @pl.when(f == pl.num_programs(1) - 1)
    def _():
        out = h_ref[...] + acc_ref[...]
        if with_final:
            out = _rms_normalize(out, fg_ref[...])
        o_ref[...] = out


def _mlp(h, g, w1, w2, final_g=None, cast=()):
    t, d = h.shape
    dff = w1.shape[1]
    tm, tf = MLP_TM, MLP_TF
    with_final = final_g is not None
    n_steps = dff // tf
    in_specs = [
        pl.BlockSpec((tm, d), lambda i, f: (i, 0)),
        pl.BlockSpec((1, d), lambda i, f: (0, 0)),
        pl.BlockSpec((d, tf), lambda i, f: (0, f)),
        pl.BlockSpec((tf, d), lambda i, f: (f, 0)),
    ]
    args = [h, g, w1, w2]
    if with_final:
        in_specs.append(pl.BlockSpec((1, d), lambda i, f: (0, 0)))
        args.append(final_g)
    out_specs = [pl.BlockSpec((tm, d), lambda i, f: (i, 0))]
    out_shape = [jax.ShapeDtypeStruct((t, d), F32)]
    for arr, lyr in cast:
        in_spec, out_spec, shape = _cast_rider(arr, lyr, (t // tm) * n_steps,
                                               lambda i, f: i * n_steps + f)
        in_specs.append(in_spec)
        out_specs.append(out_spec)
        out_shape.append(shape)
        args.append(arr)
    out = pl.pallas_call(
        functools.partial(_mlp_kernel, with_final=with_final, n_cast=len(cast)),
        grid=(t // tm, n_steps),
        in_specs=in_specs,
        out_specs=out_specs,
        out_shape=out_shape,
        scratch_shapes=[pltpu.VMEM((tm, d), BF16), pltpu.VMEM((tm, d), F32)],
        compiler_params=_params(("parallel", "arbitrary"), 56),
        name="mlp_final" if with_final else "mlp",
    )(*args)
    return out if len(out) > 1 else out[0]


def kernel(x, norm_mix_g, norm_mlp_g, even_w_in, even_b_f, even_ret_gn, even_w_out,
           odd_w_in, odd_lambda_q1, odd_lambda_k1, odd_lambda_q2, odd_lambda_k2,
           odd_subln_g, odd_w_out, mlp_w1, mlp_w2, final_g):
    batch, seq, d = x.shape
    t = batch * seq
    ret_w = N_HEADS * HEAD_DIM
    row = lambda v: v.reshape(1, -1).astype(F32)

    half = HEAD_DIM // 2
    inv = ROPE_BASE ** (-jnp.arange(half, dtype=F32) / half)
    ang = jnp.arange(seq, dtype=F32)[:, None] * inv[None, :]
    cos = jnp.concatenate([jnp.cos(ang), jnp.cos(ang)], axis=-1)
    sin_signed = jnp.concatenate([-jnp.sin(ang), jnp.sin(ang)], axis=-1)
    log_g = jnp.log1p(-(2.0 ** (-5.0 - jnp.arange(N_HEADS, dtype=F32))))

    h = x.reshape(t, d)

    n_main = 7 * ret_w
    wf = jnp.pad(even_w_in[0, :, n_main:], ((0, 0), (0, LANES - N_HEADS)))
    w_in_t = jnp.swapaxes(even_w_in, 1, 2).astype(BF16)
    z, ff, w1_0, w2_0, w_out_0 = _norm_proj(
        h, row(norm_mix_g[0]), w_in_t, 0, n_main, wf, w_transposed=True,
        cast=[(mlp_w1, 0), (mlp_w2, 0), (even_w_out, 0)])
    b_f = jnp.pad(even_b_f[0].astype(F32), (0, LANES - N_HEADS)).reshape(1, LANES)
    fbias = _forget_bias(ff, b_f, batch, seq)
    ret = _retention(z, log_g, cos, sin_signed, row(even_ret_gn[0]), batch, seq)
    fox = _fox_attention(z, fbias, batch, seq)
    h, w_in_1, w_out_1 = _proj_residual([ret, fox], w_out_0, h,
                                        cast=[(odd_w_in, 0), (odd_w_out, 0)])
    h, w1_1, w2_1 = _mlp(h, row(norm_mlp_g[0]), w1_0, w2_0,
                         cast=[(mlp_w1, 1), (mlp_w2, 1)])

    lambda_init = 0.8 - 0.6 * math.exp(-0.3 * 1)
    z = _norm_proj(h, row(norm_mix_g[1]), w_in_1[None], 0, w_in_1.shape[1])
    att = _diff_attention(z, row(odd_lambda_q1[0]), row(odd_lambda_k1[0]),
                          row(odd_lambda_q2[0]), row(odd_lambda_k2[0]),
                          row(odd_subln_g[0]), lambda_init, batch, seq)
    h = _proj_residual([att], w_out_1, h)
    h = _mlp(h, row(norm_mlp_g[1]), w1_1, w2_1, final_g=row(final_g))
    return h.reshape(batch, seq, d)
```

```python
import functools
import math

import jax
import jax.numpy as jnp
from jax import lax
from jax.experimental import pallas as pl
from jax.experimental.pallas import tpu as pltpu

F32 = jnp.float32
BF16 = jnp.bfloat16

CHUNK = 64
CHUNK_SHIFT = 6
HEAD_DIM = 128
N_HEADS = 8
ROPE_BASE = 10000.0
RMS_EPS = 1e-6
GN_EPS = 1e-5
LOG2E = 1.4426950408889634
LANES = 128
SUBLANES = 8
BF16_ROWS = 16
MIB = 1024 * 1024

MXU_COLS = 256
PROJ_TM = 1024
PROJ_TN_MAX = 2048
OUT_TM = 512
MLP_TM = 512
MLP_TF = 1024
RET_ROWS = 256
ATT_T = 512
CUMSUM_ROWS = 256


def _params(semantics, vmem_mib):
    return pltpu.CompilerParams(dimension_semantics=semantics,
                                vmem_limit_bytes=vmem_mib * MIB)


def _dot(a, b):
    return jnp.dot(a, b, preferred_element_type=F32)


def _dot_nt(a, b):
    return lax.dot_general(a, b, (((1,), (1,)), ((), ())), preferred_element_type=F32)


def _dot_tn(a, b):
    return lax.dot_general(a, b, (((0,), (0,)), ((), ())), preferred_element_type=F32)


def _rms_normalize(x, g):
    ms = jnp.mean(x * x, axis=-1, keepdims=True)
    return (x * lax.rsqrt(ms + RMS_EPS)) * g


def _slab_reduce(x, op):
    rows, n = x.shape
    return op(x.reshape(rows // SUBLANES, SUBLANES, n), axis=0)


def _cast_rider(arr, layer, n_steps, step_index):
    rows, cols = arr.shape[1:]
    slab = max(BF16_ROWS, rows // n_steps)
    share = slab * n_steps // rows
    in_spec = pl.BlockSpec((None, slab, cols),
                           lambda *ids: (layer, step_index(*ids) // share, 0))
    out_spec = pl.BlockSpec((slab, cols), lambda *ids: (step_index(*ids) // share, 0))
    return in_spec, out_spec, jax.ShapeDtypeStruct((rows, cols), BF16)


def _run_cast_riders(src_refs, dst_refs):
    for src, dst in zip(src_refs, dst_refs):
        dst[...] = src[...].astype(dst.dtype)


def _norm_proj_kernel(*refs, with_forget, w_transposed, n_cast):
    x_ref, g_ref, w_ref = refs[:3]
    refs = list(refs[3:])
    wf_ref = refs.pop(0) if with_forget else None
    cast_src = [refs.pop(0) for _ in range(n_cast)]
    o_ref = refs.pop(0)
    f_ref = refs.pop(0) if with_forget else None
    cast_dst = [refs.pop(0) for _ in range(n_cast)]
    (xn_ref,) = refs

    @pl.when(pl.program_id(1) == 0)
    def _():
        xn_ref[...] = _rms_normalize(x_ref[...], g_ref[...]).astype(BF16)
        if with_forget:
            f_ref[...] = _dot(xn_ref[...], wf_ref[...].astype(BF16))

    matmul = _dot_nt if w_transposed else _dot
    o_ref[...] = matmul(xn_ref[...], w_ref[...]).astype(o_ref.dtype)
    _run_cast_riders(cast_src, cast_dst)


def _norm_proj(x, g, w, layer, n, wf=None, w_transposed=False, cast=()):
    t, d = x.shape
    tm = PROJ_TM
    tn = max(c for c in range(MXU_COLS, PROJ_TN_MAX + 1, MXU_COLS) if n % c == 0)
    with_forget = wf is not None
    n_col_steps = n // tn
    riders = [_cast_rider(arr, lyr, (t // tm) * n_col_steps, lambda i, j: i * n_col_steps + j)
              for arr, lyr in cast]
    if w_transposed:
        w_spec = pl.BlockSpec((None, tn, d), lambda i, j: (layer, j, 0))
    else:
        w_spec = pl.BlockSpec((None, d, tn), lambda i, j: (layer, 0, j))
    in_specs = [
        pl.BlockSpec((tm, d), lambda i, j: (i, 0)),
        pl.BlockSpec((1, d), lambda i, j: (0, 0)),
        w_spec,
    ]
    out_shape = [jax.ShapeDtypeStruct((t, n), BF16)]
    out_specs = [pl.BlockSpec((tm, tn), lambda i, j: (i, j))]
    args = [x, g, w]
    if with_forget:
        in_specs.append(pl.BlockSpec((d, LANES), lambda i, j: (0, 0)))
        out_shape.append(jax.ShapeDtypeStruct((t, LANES), F32))
        out_specs.append(pl.BlockSpec((tm, LANES), lambda i, j: (i, 0)))
        args.append(wf)
    for (in_spec, out_spec, shape), (arr, _) in zip(riders, cast):
        in_specs.append(in_spec)
        out_specs.append(out_spec)
        out_shape.append(shape)
        args.append(arr)
    out = pl.pallas_call(
        functools.partial(_norm_proj_kernel, with_forget=with_forget,
                          w_transposed=w_transposed, n_cast=len(cast)),
        grid=(t // tm, n_col_steps),
        in_specs=in_specs,
        out_specs=out_specs,
        out_shape=out_shape,
        scratch_shapes=[pltpu.VMEM((tm, d), BF16)],
        compiler_params=_params(("parallel", "arbitrary"), 56),
        name="norm_proj_forget" if with_forget else "norm_proj",
    )(*args)
    return out if len(out) > 1 else out[0]


def _rotate(x, cos, sin_signed):
    return x * cos + pltpu.roll(x, HEAD_DIM // 2, 1) * sin_signed


def _retention_kernel(lg_ref, q_ref, k_ref, v_ref, gate_ref, cos_ref, sin_ref, gn_ref, o_ref):
    seq = q_ref.shape[0]
    rows = RET_ROWS
    lg = lg_ref[pl.program_id(1)]

    row = lax.broadcasted_iota(jnp.int32, (rows, HEAD_DIM), 0).astype(F32)
    xi = jnp.exp(lg * (row + 1.0))
    zeta = jnp.exp(lg * (rows - 1.0 - row))
    ti = lax.broadcasted_iota(jnp.int32, (rows, rows), 0)
    si = lax.broadcasted_iota(jnp.int32, (rows, rows), 1)
    visible = (si >> CHUNK_SHIFT) <= (ti >> CHUNK_SHIFT)
    decay = jnp.where(visible, jnp.exp(lg * jnp.abs(ti - si).astype(F32)), 0.0)
    block_decay = jnp.exp(jnp.full((1, HEAD_DIM), lg * rows, F32))
    gn = gn_ref[...]

    state = jnp.zeros((HEAD_DIM, HEAD_DIM), F32)
    for r in range(seq // rows):
        sl = slice(r * rows, (r + 1) * rows)
        cos = cos_ref[sl, :]
        sin = sin_ref[sl, :]
        q = _rotate(q_ref[sl, :].astype(F32), cos, sin) * (HEAD_DIM ** -0.5)
        k = _rotate(k_ref[sl, :].astype(F32), cos, sin)
        v = v_ref[sl, :]
        scores = _dot_nt(q.astype(BF16), k.astype(BF16)) * decay
        out = _dot(scores.astype(BF16), v) + _dot((q * xi).astype(BF16), state.astype(BF16))
        state = state * block_decay + _dot_tn((k * zeta).astype(BF16), v)

        ms = jnp.mean(out * out, axis=-1, keepdims=True)
        y = out * lax.rsqrt(ms + GN_EPS) * gn
        gate = gate_ref[sl, :].astype(F32)
        o_ref[sl, :] = (jax.nn.silu(gate) * y).astype(o_ref.dtype)


def _retention(z, log_g, cos, sin_signed, gn, batch, seq):
    t = z.shape[0]
    col = lambda off: pl.BlockSpec((seq, HEAD_DIM), lambda b, h: (b, off + h))
    tab = pl.BlockSpec((seq, HEAD_DIM), lambda b, h: (0, 0))
    return pl.pallas_call(
        _retention_kernel,
        grid=(batch, N_HEADS),
        in_specs=[
            pl.BlockSpec(memory_space=pltpu.SMEM),
            col(0), col(N_HEADS), col(2 * N_HEADS), col(3 * N_HEADS),
            tab, tab,
            pl.BlockSpec((1, HEAD_DIM), lambda b, h: (0, h)),
        ],
        out_specs=pl.BlockSpec((seq, HEAD_DIM), lambda b, h: (b, h)),
        out_shape=jax.ShapeDtypeStruct((t, N_HEADS * HEAD_DIM), BF16),
        compiler_params=_params(("parallel", "parallel"), 32),
        name="retention",
    )(log_g, z, z, z, z, cos, sin_signed, gn)


def _forget_bias_kernel(ff_ref, b_ref, tri_ref, o_ref):
    seq = ff_ref.shape[0]
    rows = tri_ref.shape[0]
    lane = lax.broadcasted_iota(jnp.int32, (rows, LANES), 1)
    carry = jnp.zeros((1, LANES), F32)
    for blk in range(seq // rows):
        sl = slice(blk * rows, (blk + 1) * rows)
        log_f = jax.nn.log_sigmoid(ff_ref[sl, :] + b_ref[...])
        c = jnp.dot(tri_ref[...], log_f, precision=lax.Precision.HIGHEST,
                    preferred_element_type=F32) + carry
        carry = c[rows - 1:rows, :]
        for h in range(N_HEADS):
            x = jnp.broadcast_to(c[:, h:h + 1], (rows, LANES)) * (-LOG2E)
            hi = x.astype(BF16).astype(F32)
            mid = (x - hi).astype(BF16).astype(F32)
            lo = x - hi - mid
            pieces = jnp.where(lane == 0, hi, jnp.where(lane == 1, mid,
                                                       jnp.where(lane == 2, lo, 0.0)))
            o_ref[h, sl, :] = pieces.astype(o_ref.dtype)


def _forget_bias(ff, b_f, batch, seq):
    rows = CUMSUM_ROWS
    tri = (jnp.arange(rows)[None, :] <= jnp.arange(rows)[:, None]).astype(F32)
    out = pl.pallas_call(
        _forget_bias_kernel,
        grid=(batch,),
        in_specs=[
            pl.BlockSpec((seq, LANES), lambda b: (b, 0)),
            pl.BlockSpec((1, LANES), lambda b: (0, 0)),
            pl.BlockSpec((rows, rows), lambda b: (0, 0)),
        ],
        out_specs=pl.BlockSpec((None, N_HEADS, seq, LANES), lambda b: (b, 0, 0, 0)),
        out_shape=jax.ShapeDtypeStruct((batch, N_HEADS, seq, LANES), BF16),
        compiler_params=_params(("parallel",), 32),
        name="forget_bias",
    )(ff, b_f, tri)
    return out.reshape(batch * N_HEADS * seq, LANES)


def _pipeline_blocks(blocks):
    for task in blocks[0][0]:
        task()
    for n, (_, value_tasks) in enumerate(blocks):
        score_tasks = blocks[n + 1][0] if n + 1 < len(blocks) else []
        done = 0
        for idx, task in enumerate(value_tasks):
            task()
            while done < len(score_tasks) and done * len(value_tasks) < (idx + 1) * len(score_tasks):
                score_tasks[done]()
                done += 1


def _diag_tile_max(s, n_full, band_rows, band_mask):
    parts = []
    for j in range(s.shape[1] // LANES):
        lanes = slice(j * LANES, (j + 1) * LANES)
        nf = n_full(j)
        band = jnp.where(band_mask, s[nf:nf + band_rows, lanes], -jnp.inf)
        mx = _slab_reduce(band, jnp.max)
        if nf:
            mx = jnp.maximum(mx, _slab_reduce(s[:nf, lanes], jnp.max))
        parts.append(mx)
    return jnp.concatenate(parts, axis=1)


def _diag_tile_exp(s_ref, row0, m, n_full, band_rows, band_mask):
    t = s_ref.shape[1]
    strips, sums = [], []
    for j in range(t // LANES):
        lanes = slice(j * LANES, (j + 1) * LANES)
        nf = n_full(j)
        nv = nf + band_rows
        pj = jnp.exp2(s_ref[row0:row0 + nv, lanes] - m[:, lanes])
        band = jnp.where(band_mask, pj[nf:, :], 0.0)
        total = _slab_reduce(band, jnp.sum)
        pieces = [band.astype(BF16)]
        if nf:
            total = total + _slab_reduce(pj[:nf, :], jnp.sum)
            pieces.insert(0, pj[:nf, :].astype(BF16))
        if nv < t:
            pieces.append(jnp.zeros((t - nv, LANES), BF16))
        strips.append(jnp.concatenate(pieces, axis=0))
        sums.append(total)
    return jnp.concatenate(strips, axis=1), jnp.concatenate(sums, axis=1)


def _fox_kernel(q_ref, k_ref, v_ref, fb_ref, o_ref, vt_ref, s_ref):
    seq = q_ref.shape[0]
    t = ATT_T
    vt_ref[...] = v_ref[...].astype(F32).T.astype(BF16)
    lane = lax.broadcasted_iota(jnp.int32, (t, HEAD_DIM), 1)
    ones3 = jnp.where(lane < 3, 1.0, 0.0).astype(BF16)
    band_mask = (lax.broadcasted_iota(jnp.int32, (LANES, LANES), 0)
                 <= lax.broadcasted_iota(jnp.int32, (LANES, LANES), 1))
    n_full = lambda j: j * LANES
    qscale = (HEAD_DIM ** -0.5) * LOG2E

    def query_block(i):
        rows_i = slice(i * t, (i + 1) * t)
        sbuf = s_ref.at[i % 2]
        st = {"m_run": None, "l_run": jnp.zeros((SUBLANES, t), F32),
              "acc": jnp.zeros((HEAD_DIM, t), F32)}

        def score_task(c):
            rows_c = slice(c * t, (c + 1) * t)
            if c == 0:
                qs = (q_ref[rows_i, :].astype(F32) * qscale).astype(BF16)
                st["qa"] = jnp.concatenate([qs, ones3], axis=1)
            ka = jnp.concatenate([k_ref[rows_c, :], fb_ref[rows_c, :]], axis=1)
            s = _dot_nt(ka, st["qa"])
            sbuf[rows_c, :] = s
            if c == i:
                tile_max = _diag_tile_max(s, n_full, LANES, band_mask)
            else:
                tile_max = _slab_reduce(s, jnp.max)
            st["m_run"] = tile_max if c == 0 else jnp.maximum(st["m_run"], tile_max)

        def value_task(c):
            rows_c = slice(c * t, (c + 1) * t)
            if c == 0:
                st["m"] = jnp.max(st["m_run"], axis=0, keepdims=True)
            if c == i:
                p, sums = _diag_tile_exp(sbuf, c * t, st["m"], n_full, LANES, band_mask)
            else:
                pf = jnp.exp2(sbuf[rows_c, :] - st["m"])
                p, sums = pf.astype(BF16), _slab_reduce(pf, jnp.sum)
            st["l_run"] = st["l_run"] + sums
            st["acc"] = st["acc"] + _dot(vt_ref[:, rows_c], p)
            if c == i:
                l = jnp.sum(st["l_run"], axis=0, keepdims=True)
                o_ref[rows_i, :] = (st["acc"] / l).T.astype(o_ref.dtype)

        tiles = range(i + 1)
        return ([functools.partial(score_task, c) for c in tiles],
                [functools.partial(value_task, c) for c in tiles])

    _pipeline_blocks([query_block(i) for i in range(seq // t)])


def _fox_attention(z, fbias, batch, seq):
    t = z.shape[0]
    col = lambda off: pl.BlockSpec((seq, HEAD_DIM), lambda b, h: (b, off + h))
    return pl.pallas_call(
        _fox_kernel,
        grid=(batch, N_HEADS),
        in_specs=[
            col(4 * N_HEADS), col(5 * N_HEADS), col(6 * N_HEADS),
            pl.BlockSpec((seq, LANES), lambda b, h: (b * N_HEADS + h, 0)),
        ],
        out_specs=pl.BlockSpec((seq, HEAD_DIM), lambda b, h: (b, h)),
        out_shape=jax.ShapeDtypeStruct((t, N_HEADS * HEAD_DIM), BF16),
        scratch_shapes=[pltpu.VMEM((HEAD_DIM, seq), BF16),
                        pltpu.VMEM((2, seq, ATT_T), F32)],
        compiler_params=_params(("parallel", "parallel"), 40),
        name="fox_attention",
    )(z, z, z, fbias)


def _diff_kernel(q_ref, k_ref, v_ref, lq1_ref, lk1_ref, lq2_ref, lk2_ref, g_ref, o_ref,
                 vt_ref, s1_ref, s2_ref, p1_ref, p2_ref, acc_ref, *, lambda_init):
    seq = q_ref.shape[0]
    t = ATT_T
    vt_ref[...] = v_ref[...].astype(F32).T.astype(BF16)
    lam = (jnp.exp(jnp.sum(lq1_ref[...] * lk1_ref[...], axis=-1, keepdims=True))
           - jnp.exp(jnp.sum(lq2_ref[...] * lk2_ref[...], axis=-1, keepdims=True))
           + lambda_init)
    band_mask = lax.broadcasted_iota(jnp.int32, (CHUNK, LANES), 1) >= CHUNK
    n_full = lambda j: j * LANES + CHUNK
    qscale = (HEAD_DIM ** -0.5) * LOG2E
    col_slices = (slice(0, HEAD_DIM), slice(HEAD_DIM, 2 * HEAD_DIM))
    p_refs = (p1_ref, p2_ref)

    def query_block(i):
        rows_i = slice(i * t, (i + 1) * t)
        s_refs = (s1_ref.at[i % 2], s2_ref.at[i % 2])
        zero = jnp.zeros((SUBLANES, t), F32)
        st = {"m_run": [None, None], "m": [None, None], "l_run": [zero, zero]}

        def score_task(mi, c):
            rows_c = slice(c * t, (c + 1) * t)
            if mi == 0 and c == 0:
                st["qs"] = (q_ref[rows_i, :].astype(F32) * qscale).astype(BF16)
            cols = col_slices[mi]
            s = _dot_nt(k_ref[rows_c, cols], st["qs"][:, cols])
            s_refs[mi][rows_c, :] = s
            if c == i:
                tile_max = _diag_tile_max(s, n_full, CHUNK, band_mask)
            else:
                tile_max = _slab_reduce(s, jnp.max)
            st["m_run"][mi] = tile_max if c == 0 else jnp.maximum(st["m_run"][mi], tile_max)

        def exp_task(mi, c):
            rows_c = slice(c * t, (c + 1) * t)
            if c == 0:
                st["m"][mi] = jnp.max(st["m_run"][mi], axis=0, keepdims=True)
            if c == i:
                p, sums = _diag_tile_exp(s_refs[mi], c * t, st["m"][mi], n_full, CHUNK,
                                         band_mask)
            else:
                pf = jnp.exp2(s_refs[mi][rows_c, :] - st["m"][mi])
                p, sums = pf.astype(BF16), _slab_reduce(pf, jnp.sum)
            p_refs[mi][rows_c, :] = p
            st["l_run"][mi] = st["l_run"][mi] + sums

        def value_task(c):
            rows_c = slice(c * t, (c + 1) * t)
            if c == 0:
                st["l1"] = jnp.sum(st["l_run"][0], axis=0, keepdims=True)
                l2 = jnp.sum(st["l_run"][1], axis=0, keepdims=True)
                st["rho"] = (lam * st["l1"] / l2).astype(BF16)
            a = p1_ref[rows_c, :] - p2_ref[rows_c, :] * st["rho"]
            av = _dot(vt_ref[:, rows_c], a)
            acc_ref[...] = av if c == 0 else acc_ref[...] + av
            if c == i:
                o = (acc_ref[...] / st["l1"]).T
                ms = jnp.mean(o * o, axis=-1, keepdims=True)
                y = o * lax.rsqrt(ms + GN_EPS) * g_ref[...] * (1.0 - lambda_init)
                o_ref[rows_i, :] = y.astype(o_ref.dtype)

        tiles = range(i + 1)
        scores = [functools.partial(score_task, mi, c) for mi in (0, 1) for c in tiles]
        values = ([functools.partial(exp_task, mi, c) for mi in (0, 1) for c in tiles]
                  + [functools.partial(value_task, c) for c in tiles])
        return scores, values

    _pipeline_blocks([query_block(i) for i in range(seq // t)])


def _diff_attention(z, lq1, lk1, lq2, lk2, subln_g, lambda_init, batch, seq):
    t = z.shape[0]
    width = 2 * HEAD_DIM
    col = lambda off: pl.BlockSpec((seq, width), lambda b, h: (b, off + h))
    vec = pl.BlockSpec((1, HEAD_DIM), lambda b, h: (0, 0))
    return pl.pallas_call(
        functools.partial(_diff_kernel, lambda_init=lambda_init),
        grid=(batch, N_HEADS),
        in_specs=[
            col(0), col(N_HEADS), col(2 * N_HEADS),
            vec, vec, vec, vec,
            pl.BlockSpec((1, width), lambda b, h: (0, 0)),
        ],
        out_specs=pl.BlockSpec((seq, width), lambda b, h: (b, h)),
        out_shape=jax.ShapeDtypeStruct((t, N_HEADS * width), BF16),
        scratch_shapes=[pltpu.VMEM((width, seq), BF16),
                        pltpu.VMEM((2, seq, ATT_T), F32),
                        pltpu.VMEM((2, seq, ATT_T), F32),
                        pltpu.VMEM((seq, ATT_T), BF16),
                        pltpu.VMEM((seq, ATT_T), BF16),
                        pltpu.VMEM((width, ATT_T), F32)],
        compiler_params=_params(("parallel", "parallel"), 48),
        name="diff_attention",
    )(z, z, z, lq1, lk1, lq2, lk2, subln_g)


def _proj_residual_kernel(*refs, n_parts, n_cast):
    refs = list(refs)
    a_refs = [refs.pop(0) for _ in range(n_parts)]
    w_refs = [refs.pop(0) for _ in range(n_parts)]
    res_ref = refs.pop(0)
    cast_src = [refs.pop(0) for _ in range(n_cast)]
    o_ref = refs.pop(0)
    cast_dst = refs
    acc = res_ref[...]
    for a_ref, w_ref in zip(a_refs, w_refs):
        acc = acc + _dot(a_ref[...], w_ref[...])
    o_ref[...] = acc
    _run_cast_riders(cast_src, cast_dst)


def _proj_residual(parts, w, res, cast=()):
    t, n = res.shape
    tm = OUT_TM
    kp = parts[0].shape[1]
    n_parts = len(parts)
    in_specs = [pl.BlockSpec((tm, kp), lambda i: (i, 0)) for _ in parts]
    in_specs += [pl.BlockSpec((kp, n), functools.partial(lambda i, p: (p, 0), p=p))
                 for p in range(n_parts)]
    in_specs.append(pl.BlockSpec((tm, n), lambda i: (i, 0)))
    args = [*parts, *([w] * n_parts), res]
    out_specs = [pl.BlockSpec((tm, n), lambda i: (i, 0))]
    out_shape = [jax.ShapeDtypeStruct((t, n), F32)]
    for arr, lyr in cast:
        in_spec, out_spec, shape = _cast_rider(arr, lyr, t // tm, lambda i: i)
        in_specs.append(in_spec)
        out_specs.append(out_spec)
        out_shape.append(shape)
        args.append(arr)
    out = pl.pallas_call(
        functools.partial(_proj_residual_kernel, n_parts=n_parts, n_cast=len(cast)),
        grid=(t // tm,),
        in_specs=in_specs,
        out_specs=out_specs,
        out_shape=out_shape,
        compiler_params=_params(("parallel",), 48),
        name="proj_residual",
    )(*args)
    return out if len(out) > 1 else out[0]


def _mlp_kernel(h_ref, g_ref, w1_ref, w2_ref, *refs, with_final, n_cast):
    refs = list(refs)
    fg_ref = refs.pop(0) if with_final else None
    cast_src = [refs.pop(0) for _ in range(n_cast)]
    o_ref = refs.pop(0)
    cast_dst = [refs.pop(0) for _ in range(n_cast)]
    xn_ref, acc_ref = refs
    f = pl.program_id(1)

    @pl.when(f == 0)
    def _():
        xn_ref[...] = _rms_normalize(h_ref[...], g_ref[...]).astype(BF16)
        acc_ref[...] = jnp.zeros_like(acc_ref)

    a = jnp.maximum(_dot(xn_ref[...], w1_ref[...]), 0.0)
    acc_ref[...] += _dot((a * a).astype(BF16), w2_ref[...])
    _run_cast_riders(cast_src, cast_dst)

    @pl.when(f == pl.num_programs(1) - 1)
    def _():
        out = h_ref[...] + acc_ref[...]
        if with_final:
            out = _rms_normalize(out, fg_ref[...])
        o_ref[...] = out


def _mlp(h, g, w1, w2, final_g=None, cast=()):
    t, d = h.shape
    dff = w1.shape[1]
    tm, tf = MLP_TM, MLP_TF
    with_final = final_g is not None
    n_steps = dff // tf
    in_specs = [
        pl.BlockSpec((tm, d), lambda i, f: (i, 0)),
        pl.BlockSpec((1, d), lambda i, f: (0, 0)),
        pl.BlockSpec((d, tf), lambda i, f: (0, f)),
        pl.BlockSpec((tf, d), lambda i, f: (f, 0)),
    ]
    args = [h, g, w1, w2]
    if with_final:
        in_specs.append(pl.BlockSpec((1, d), lambda i, f: (0, 0)))
        args.append(final_g)
    out_specs = [pl.BlockSpec((tm, d), lambda i, f: (i, 0))]
    out_shape = [jax.ShapeDtypeStruct((t, d), F32)]
    for arr, lyr in cast:
        in_spec, out_spec, shape = _cast_rider(arr, lyr, (t // tm) * n_steps,
                                               lambda i, f: i * n_steps + f)
        in_specs.append(in_spec)
        out_specs.append(out_spec)
        out_shape.append(shape)
        args.append(arr)
    out = pl.pallas_call(
        functools.partial(_mlp_kernel, with_final=with_final, n_cast=len(cast)),
        grid=(t // tm, n_steps),
        in_specs=in_specs,
        out_specs=out_specs,
        out_shape=out_shape,
        scratch_shapes=[pltpu.VMEM((tm, d), BF16), pltpu.VMEM((tm, d), F32)],
        compiler_params=_params(("parallel", "arbitrary"), 56),
        name="mlp_final" if with_final else "mlp",
    )(*args)
    return out if len(out) > 1 else out[0]


def kernel(x, norm_mix_g, norm_mlp_g, even_w_in, even_b_f, even_ret_gn, even_w_out,
           odd_w_in, odd_lambda_q1, odd_lambda_k1, odd_lambda_q2, odd_lambda_k2,
           odd_subln_g, odd_w_out, mlp_w1, mlp_w2, final_g):
    batch, seq, d = x.shape
    t = batch * seq
    ret_w = N_HEADS * HEAD_DIM
    row = lambda v: v.reshape(1, -1).astype(F32)

    half = HEAD_DIM // 2
    inv = ROPE_BASE ** (-jnp.arange(half, dtype=F32) / half)
    ang = jnp.arange(seq, dtype=F32)[:, None] * inv[None, :]
    cos = jnp.concatenate([jnp.cos(ang), jnp.cos(ang)], axis=-1)
    sin_signed = jnp.concatenate([-jnp.sin(ang), jnp.sin(ang)], axis=-1)
    log_g = jnp.log1p(-(2.0 ** (-5.0 - jnp.arange(N_HEADS, dtype=F32))))

    h = x.reshape(t, d)

    n_main = 7 * ret_w
    wf = jnp.pad(even_w_in[0, :, n_main:], ((0, 0), (0, LANES - N_HEADS)))
    w_in_t = jnp.swapaxes(even_w_in, 1, 2).astype(BF16)
    z, ff, w1_0, w2_0, w_out_0 = _norm_proj(
        h, row(norm_mix_g[0]), w_in_t, 0, n_main, wf, w_transposed=True,
        cast=[(mlp_w1, 0), (mlp_w2, 0), (even_w_out, 0)])
    b_f = jnp.pad(even_b_f[0].astype(F32), (0, LANES - N_HEADS)).reshape(1, LANES)
    fbias = _forget_bias(ff, b_f, batch, seq)
    ret = _retention(z, log_g, cos, sin_signed, row(even_ret_gn[0]), batch, seq)
    fox = _fox_attention(z, fbias, batch, seq)
    h = _proj_residual([ret, fox], w_out_0, h)
    h, w1_1, w2_1, w_in_1, w_out_1 = _mlp(
        h, row(norm_mlp_g[0]), w1_0, w2_0,
        cast=[(mlp_w1, 1), (mlp_w2, 1), (odd_w_in, 0), (odd_w_out, 0)])

    lambda_init = 0.8 - 0.6 * math.exp(-0.3 * 1)
    z = _norm_proj(h, row(norm_mix_g[1]), w_in_1[None], 0, w_in_1.shape[1])
    att = _diff_attention(z, row(odd_lambda_q1[0]), row(odd_lambda_k1[0]),
                          row(odd_lambda_q2[0]), row(odd_lambda_k2[0]),
                          row(odd_subln_g[0]), lambda_init, batch, seq)
    h = _proj_residual([att], w_out_1, h)
    h = _mlp(h, row(norm_mlp_g[1]), w1_1, w2_1, final_g=row(final_g))
    return h.reshape(batch, seq, d)
```

```python
import functools
import math

import jax
import jax.numpy as jnp
from jax import lax
from jax.experimental import pallas as pl
from jax.experimental.pallas import tpu as pltpu

F32 = jnp.float32
BF16 = jnp.bfloat16

CHUNK = 64
CHUNK_SHIFT = 6
HEAD_DIM = 128
N_HEADS = 8
ROPE_BASE = 10000.0
RMS_EPS = 1e-6
GN_EPS = 1e-5
LOG2E = 1.4426950408889634
LANES = 128
SUBLANES = 8
BF16_ROWS = 16
MIB = 1024 * 1024

MXU_COLS = 256
PROJ_TM = 1024
PROJ_TN_MAX = 2048
OUT_TM = 512
MLP_TM = 512
MLP_TF = 1024
RET_ROWS = 256
ATT_T = 512
CUMSUM_ROWS = 256


def _params(semantics, vmem_mib):
    return pltpu.CompilerParams(dimension_semantics=semantics,
                                vmem_limit_bytes=vmem_mib * MIB)


def _dot(a, b):
    return jnp.dot(a, b, preferred_element_type=F32)


def _dot_nt(a, b):
    return lax.dot_general(a, b, (((1,), (1,)), ((), ())), preferred_element_type=F32)


def _dot_tn(a, b):
    return lax.dot_general(a, b, (((0,), (0,)), ((), ())), preferred_element_type=F32)


def _rms_normalize(x, g):
    ms = jnp.mean(x * x, axis=-1, keepdims=True)
    return (x * lax.rsqrt(ms + RMS_EPS)) * g


def _slab_reduce(x, op):
    rows, n = x.shape
    return op(x.reshape(rows // SUBLANES, SUBLANES, n), axis=0)


def _cast_rider(arr, layer, n_steps, step_index):
    rows, cols = arr.shape[1:]
    slab = max(BF16_ROWS, rows // n_steps)
    share = slab * n_steps // rows
    in_spec = pl.BlockSpec((None, slab, cols),
                           lambda *ids: (layer, step_index(*ids) // share, 0))
    out_spec = pl.BlockSpec((slab, cols), lambda *ids: (step_index(*ids) // share, 0))
    return in_spec, out_spec, jax.ShapeDtypeStruct((rows, cols), BF16)


def _run_cast_riders(src_refs, dst_refs):
    for src, dst in zip(src_refs, dst_refs):
        dst[...] = src[...].astype(dst.dtype)


def _norm_proj_kernel(*refs, with_forget, w_transposed, n_cast):
    x_ref, g_ref, w_ref = refs[:3]
    refs = list(refs[3:])
    wf_ref = refs.pop(0) if with_forget else None
    cast_src = [refs.pop(0) for _ in range(n_cast)]
    o_ref = refs.pop(0)
    f_ref = refs.pop(0) if with_forget else None
    cast_dst = [refs.pop(0) for _ in range(n_cast)]
    (xn_ref,) = refs

    @pl.when(pl.program_id(1) == 0)
    def _():
        xn_ref[...] = _rms_normalize(x_ref[...], g_ref[...]).astype(BF16)
        if with_forget:
            f_ref[...] = _dot(xn_ref[...], wf_ref[...].astype(BF16))

    matmul = _dot_nt if w_transposed else _dot
    res = matmul(xn_ref[...], w_ref[...]).astype(o_ref.dtype)
    group = o_ref.shape[2]
    for c in range(o_ref.shape[0]):
        o_ref[c] = res[:, c * group:(c + 1) * group]
    _run_cast_riders(cast_src, cast_dst)


def _norm_proj(x, g, w, layer, n, group, wf=None, w_transposed=False, cast=()):
    t, d = x.shape
    tm = PROJ_TM
    tn = max(c for c in range(MXU_COLS, PROJ_TN_MAX + 1, MXU_COLS) if n % c == 0)
    with_forget = wf is not None
    n_col_steps = n // tn
    riders = [_cast_rider(arr, lyr, (t // tm) * n_col_steps, lambda i, j: i * n_col_steps + j)
              for arr, lyr in cast]
    if w_transposed:
        w_spec = pl.BlockSpec((None, tn, d), lambda i, j: (layer, j, 0))
    else:
        w_spec = pl.BlockSpec((None, d, tn), lambda i, j: (layer, 0, j))
    in_specs = [
        pl.BlockSpec((tm, d), lambda i, j: (i, 0)),
        pl.BlockSpec((1, d), lambda i, j: (0, 0)),
        w_spec,
    ]
    out_shape = [jax.ShapeDtypeStruct((n // group, t, group), BF16)]
    out_specs = [pl.BlockSpec((tn // group, tm, group), lambda i, j: (j, i, 0))]
    args = [x, g, w]
    if with_forget:
        in_specs.append(pl.BlockSpec((d, LANES), lambda i, j: (0, 0)))
        out_shape.append(jax.ShapeDtypeStruct((t, LANES), F32))
        out_specs.append(pl.BlockSpec((tm, LANES), lambda i, j: (i, 0)))
        args.append(wf)
    for (in_spec, out_spec, shape), (arr, _) in zip(riders, cast):
        in_specs.append(in_spec)
        out_specs.append(out_spec)
        out_shape.append(shape)
        args.append(arr)
    out = pl.pallas_call(
        functools.partial(_norm_proj_kernel, with_forget=with_forget,
                          w_transposed=w_transposed, n_cast=len(cast)),
        grid=(t // tm, n_col_steps),
        in_specs=in_specs,
        out_specs=out_specs,
        out_shape=out_shape,
        scratch_shapes=[pltpu.VMEM((tm, d), BF16)],
        compiler_params=_params(("parallel", "arbitrary"), 56),
        name="norm_proj_forget" if with_forget else "norm_proj",
    )(*args)
    return out if len(out) > 1 else out[0]


def _rotate(x, cos, sin_signed):
    return x * cos + pltpu.roll(x, HEAD_DIM // 2, 1) * sin_signed


def _retention_kernel(lg_ref, q_ref, k_ref, v_ref, gate_ref, cos_ref, sin_ref, gn_ref, o_ref):
    seq = q_ref.shape[0]
    rows = RET_ROWS
    lg = lg_ref[pl.program_id(1)]

    row = lax.broadcasted_iota(jnp.int32, (rows, HEAD_DIM), 0).astype(F32)
    xi = jnp.exp(lg * (row + 1.0))
    zeta = jnp.exp(lg * (rows - 1.0 - row))
    ti = lax.broadcasted_iota(jnp.int32, (rows, rows), 0)
    si = lax.broadcasted_iota(jnp.int32, (rows, rows), 1)
    visible = (si >> CHUNK_SHIFT) <= (ti >> CHUNK_SHIFT)
    decay = jnp.where(visible, jnp.exp(lg * jnp.abs(ti - si).astype(F32)), 0.0)
    block_decay = jnp.exp(jnp.full((1, HEAD_DIM), lg * rows, F32))
    gn = gn_ref[...]

    state = jnp.zeros((HEAD_DIM, HEAD_DIM), F32)
    for r in range(seq // rows):
        sl = slice(r * rows, (r + 1) * rows)
        cos = cos_ref[sl, :]
        sin = sin_ref[sl, :]
        q = _rotate(q_ref[sl, :].astype(F32), cos, sin) * (HEAD_DIM ** -0.5)
        k = _rotate(k_ref[sl, :].astype(F32), cos, sin)
        v = v_ref[sl, :]
        scores = _dot_nt(q.astype(BF16), k.astype(BF16)) * decay
        out = _dot(scores.astype(BF16), v) + _dot((q * xi).astype(BF16), state.astype(BF16))
        state = state * block_decay + _dot_tn((k * zeta).astype(BF16), v)

        ms = jnp.mean(out * out, axis=-1, keepdims=True)
        y = out * lax.rsqrt(ms + GN_EPS) * gn
        gate = gate_ref[sl, :].astype(F32)
        o_ref[sl, :] = (jax.nn.silu(gate) * y).astype(o_ref.dtype)


def _retention(z, log_g, cos, sin_signed, gn, batch, seq):
    t = z.shape[1]
    col = lambda off: pl.BlockSpec((None, seq, HEAD_DIM), lambda b, h: (off + h, b, 0))
    tab = pl.BlockSpec((seq, HEAD_DIM), lambda b, h: (0, 0))
    return pl.pallas_call(
        _retention_kernel,
        grid=(batch, N_HEADS),
        in_specs=[
            pl.BlockSpec(memory_space=pltpu.SMEM),
            col(0), col(N_HEADS), col(2 * N_HEADS), col(3 * N_HEADS),
            tab, tab,
            pl.BlockSpec((1, HEAD_DIM), lambda b, h: (0, h)),
        ],
        out_specs=pl.BlockSpec((seq, HEAD_DIM), lambda b, h: (b, h)),
        out_shape=jax.ShapeDtypeStruct((t, N_HEADS * HEAD_DIM), BF16),
        compiler_params=_params(("parallel", "parallel"), 32),
        name="retention",
    )(log_g, z, z, z, z, cos, sin_signed, gn)


def _forget_bias_kernel(ff_ref, b_ref, tri_ref, o_ref):
    seq = ff_ref.shape[0]
    rows = tri_ref.shape[0]
    lane = lax.broadcasted_iota(jnp.int32, (rows, LANES), 1)
    carry = jnp.zeros((1, LANES), F32)
    for blk in range(seq // rows):
        sl = slice(blk * rows, (blk + 1) * rows)
        log_f = jax.nn.log_sigmoid(ff_ref[sl, :] + b_ref[...])
        c = jnp.dot(tri_ref[...], log_f, precision=lax.Precision.HIGHEST,
                    preferred_element_type=F32) + carry
        carry = c[rows - 1:rows, :]
        for h in range(N_HEADS):
            x = jnp.broadcast_to(c[:, h:h + 1], (rows, LANES)) * (-LOG2E)
            hi = x.astype(BF16).astype(F32)
            mid = (x - hi).astype(BF16).astype(F32)
            lo = x - hi - mid
            pieces = jnp.where(lane == 0, hi, jnp.where(lane == 1, mid,
                                                       jnp.where(lane == 2, lo, 0.0)))
            o_ref[h, sl, :] = pieces.astype(o_ref.dtype)


def _forget_bias(ff, b_f, batch, seq):
    rows = CUMSUM_ROWS
    tri = (jnp.arange(rows)[None, :] <= jnp.arange(rows)[:, None]).astype(F32)
    out = pl.pallas_call(
        _forget_bias_kernel,
        grid=(batch,),
        in_specs=[
            pl.BlockSpec((seq, LANES), lambda b: (b, 0)),
            pl.BlockSpec((1, LANES), lambda b: (0, 0)),
            pl.BlockSpec((rows, rows), lambda b: (0, 0)),
        ],
        out_specs=pl.BlockSpec((None, N_HEADS, seq, LANES), lambda b: (b, 0, 0, 0)),
        out_shape=jax.ShapeDtypeStruct((batch, N_HEADS, seq, LANES), BF16),
        compiler_params=_params(("parallel",), 32),
        name="forget_bias",
    )(ff, b_f, tri)
    return out.reshape(batch * N_HEADS * seq, LANES)


def _pipeline_blocks(blocks):
    n_stages = len(blocks[0])
    for slot in range(len(blocks) + n_stages - 1):
        keyed = []
        for j in range(n_stages):
            if 0 <= slot - j < len(blocks):
                tasks = blocks[slot - j][j]
                keyed += [((idx + 0.5) / len(tasks), -j, idx, task)
                          for idx, task in enumerate(tasks)]
        for _, _, _, task in sorted(keyed, key=lambda e: e[:3]):
            task()


def _diag_tile_max(s, n_full, band_rows, band_mask):
    parts = []
    for j in range(s.shape[1] // LANES):
        lanes = slice(j * LANES, (j + 1) * LANES)
        nf = n_full(j)
        band = jnp.where(band_mask, s[nf:nf + band_rows, lanes], -jnp.inf)
        mx = _slab_reduce(band, jnp.max)
        if nf:
            mx = jnp.maximum(mx, _slab_reduce(s[:nf, lanes], jnp.max))
        parts.append(mx)
    return jnp.concatenate(parts, axis=1)


def _diag_tile_exp(s_ref, row0, m, n_full, band_rows, band_mask):
    t = s_ref.shape[1]
    strips, sums = [], []
    for j in range(t // LANES):
        lanes = slice(j * LANES, (j + 1) * LANES)
        nf = n_full(j)
        nv = nf + band_rows
        pj = jnp.exp2(s_ref[row0:row0 + nv, lanes] - m[:, lanes])
        band = jnp.where(band_mask, pj[nf:, :], 0.0)
        total = _slab_reduce(band, jnp.sum)
        pieces = [band.astype(BF16)]
        if nf:
            total = total + _slab_reduce(pj[:nf, :], jnp.sum)
            pieces.insert(0, pj[:nf, :].astype(BF16))
        if nv < t:
            pieces.append(jnp.zeros((t - nv, LANES), BF16))
        strips.append(jnp.concatenate(pieces, axis=0))
        sums.append(total)
    return jnp.concatenate(strips, axis=1), jnp.concatenate(sums, axis=1)


def _fox_kernel(q_ref, k_ref, v_ref, fb_ref, o_ref, vt_ref, s_ref):
    seq = q_ref.shape[0]
    t = ATT_T
    vt_ref[...] = v_ref[...].astype(F32).T.astype(BF16)
    lane = lax.broadcasted_iota(jnp.int32, (t, HEAD_DIM), 1)
    ones3 = jnp.where(lane < 3, 1.0, 0.0).astype(BF16)
    band_mask = (lax.broadcasted_iota(jnp.int32, (LANES, LANES), 0)
                 <= lax.broadcasted_iota(jnp.int32, (LANES, LANES), 1))
    n_full = lambda j: j * LANES
    qscale = (HEAD_DIM ** -0.5) * LOG2E

    def query_block(i):
        rows_i = slice(i * t, (i + 1) * t)
        sbuf = s_ref.at[i % 2]
        st = {"m_run": None, "l_run": jnp.zeros((SUBLANES, t), F32),
              "acc": jnp.zeros((HEAD_DIM, t), F32)}

        def score_task(c):
            rows_c = slice(c * t, (c + 1) * t)
            if c == 0:
                qs = (q_ref[rows_i, :].astype(F32) * qscale).astype(BF16)
                st["qa"] = jnp.concatenate([qs, ones3], axis=1)
            ka = jnp.concatenate([k_ref[rows_c, :], fb_ref[rows_c, :]], axis=1)
            s = _dot_nt(ka, st["qa"])
            sbuf[rows_c, :] = s
            if c == i:
                tile_max = _diag_tile_max(s, n_full, LANES, band_mask)
            else:
                tile_max = _slab_reduce(s, jnp.max)
            st["m_run"] = tile_max if c == 0 else jnp.maximum(st["m_run"], tile_max)

        def value_task(c):
            rows_c = slice(c * t, (c + 1) * t)
            if c == 0:
                st["m"] = jnp.max(st["m_run"], axis=0, keepdims=True)
            if c == i:
                p, sums = _diag_tile_exp(sbuf, c * t, st["m"], n_full, LANES, band_mask)
            else:
                pf = jnp.exp2(sbuf[rows_c, :] - st["m"])
                p, sums = pf.astype(BF16), _slab_reduce(pf, jnp.sum)
            st["l_run"] = st["l_run"] + sums
            st["acc"] = st["acc"] + _dot(vt_ref[:, rows_c], p)
            if c == i:
                l = jnp.sum(st["l_run"], axis=0, keepdims=True)
                o_ref[rows_i, :] = (st["acc"] / l).T.astype(o_ref.dtype)

        tiles = range(i + 1)
        return ([functools.partial(score_task, c) for c in tiles],
                [functools.partial(value_task, c) for c in tiles])

    _pipeline_blocks([query_block(i) for i in range(seq // t)])


def _fox_attention(z, fbias, batch, seq):
    t = z.shape[1]
    col = lambda off: pl.BlockSpec((None, seq, HEAD_DIM), lambda b, h: (off + h, b, 0))
    return pl.pallas_call(
        _fox_kernel,
        grid=(batch, N_HEADS),
        in_specs=[
            col(4 * N_HEADS), col(5 * N_HEADS), col(6 * N_HEADS),
            pl.BlockSpec((seq, LANES), lambda b, h: (b * N_HEADS + h, 0)),
        ],
        out_specs=pl.BlockSpec((seq, HEAD_DIM), lambda b, h: (b, h)),
        out_shape=jax.ShapeDtypeStruct((t, N_HEADS * HEAD_DIM), BF16),
        scratch_shapes=[pltpu.VMEM((HEAD_DIM, seq), BF16),
                        pltpu.VMEM((2, seq, ATT_T), F32)],
        compiler_params=_params(("parallel", "parallel"), 40),
        name="fox_attention",
    )(z, z, z, fbias)


def _diff_kernel(q_ref, k_ref, v_ref, lq1_ref, lk1_ref, lq2_ref, lk2_ref, g_ref, o_ref,
                 vt_ref, s1_ref, s2_ref, p1_ref, p2_ref, acc_ref, *, lambda_init):
    seq = q_ref.shape[0]
    t = ATT_T
    vt_ref[...] = v_ref[...].astype(F32).T.astype(BF16)
    lam = (jnp.exp(jnp.sum(lq1_ref[...] * lk1_ref[...], axis=-1, keepdims=True))
           - jnp.exp(jnp.sum(lq2_ref[...] * lk2_ref[...], axis=-1, keepdims=True))
           + lambda_init)
    band_mask = lax.broadcasted_iota(jnp.int32, (CHUNK, LANES), 1) >= CHUNK
    n_full = lambda j: j * LANES + CHUNK
    qscale = (HEAD_DIM ** -0.5) * LOG2E
    col_slices = (slice(0, HEAD_DIM), slice(HEAD_DIM, 2 * HEAD_DIM))
    def query_block(i):
        rows_i = slice(i * t, (i + 1) * t)
        s_refs = (s1_ref.at[i % 2], s2_ref.at[i % 2])
        p_refs = (p1_ref, p2_ref)
        zero = jnp.zeros((SUBLANES, t), F32)
        st = {"m_run": [None, None], "m": [None, None], "l_run": [zero, zero]}

        def score_task(mi, c):
            rows_c = slice(c * t, (c + 1) * t)
            if mi == 0 and c == 0:
                st["qs"] = (q_ref[rows_i, :].astype(F32) * qscale).astype(BF16)
            cols = col_slices[mi]
            s = _dot_nt(k_ref[rows_c, cols], st["qs"][:, cols])
            s_refs[mi][rows_c, :] = s
            if c == i:
                tile_max = _diag_tile_max(s, n_full, CHUNK, band_mask)
            else:
                tile_max = _slab_reduce(s, jnp.max)
            st["m_run"][mi] = tile_max if c == 0 else jnp.maximum(st["m_run"][mi], tile_max)

        def exp_task(mi, c):
            rows_c = slice(c * t, (c + 1) * t)
            if c == 0:
                st["m"][mi] = jnp.max(st["m_run"][mi], axis=0, keepdims=True)
            if c == i:
                p, sums = _diag_tile_exp(s_refs[mi], c * t, st["m"][mi], n_full, CHUNK,
                                         band_mask)
            else:
                pf = jnp.exp2(s_refs[mi][rows_c, :] - st["m"][mi])
                p, sums = pf.astype(BF16), _slab_reduce(pf, jnp.sum)
            p_refs[mi][rows_c, :] = p
            st["l_run"][mi] = st["l_run"][mi] + sums

        def value_task(c):
            rows_c = slice(c * t, (c + 1) * t)
            if c == 0:
                st["l1"] = jnp.sum(st["l_run"][0], axis=0, keepdims=True)
                l2 = jnp.sum(st["l_run"][1], axis=0, keepdims=True)
                st["rho"] = (lam * st["l1"] / l2).astype(BF16)
            a = p_refs[0][rows_c, :] - p_refs[1][rows_c, :] * st["rho"]
            av = _dot(vt_ref[:, rows_c], a)
            acc_ref[...] = av if c == 0 else acc_ref[...] + av
            if c == i:
                o = (acc_ref[...] / st["l1"]).T
                ms = jnp.mean(o * o, axis=-1, keepdims=True)
                y = o * lax.rsqrt(ms + GN_EPS) * g_ref[...] * (1.0 - lambda_init)
                o_ref[rows_i, :] = y.astype(o_ref.dtype)

        tiles = range(i + 1)
        return ([functools.partial(score_task, mi, c) for mi in (0, 1) for c in tiles],
                [functools.partial(exp_task, mi, c) for mi in (0, 1) for c in tiles]
                + [functools.partial(value_task, c) for c in tiles])

    _pipeline_blocks([query_block(i) for i in range(seq // t)])


def _diff_attention(z, lq1, lk1, lq2, lk2, subln_g, lambda_init, batch, seq):
    t = z.shape[1]
    width = 2 * HEAD_DIM
    col = lambda off: pl.BlockSpec((None, seq, width), lambda b, h: (off + h, b, 0))
    vec = pl.BlockSpec((1, HEAD_DIM), lambda b, h: (0, 0))
    return pl.pallas_call(
        functools.partial(_diff_kernel, lambda_init=lambda_init),
        grid=(batch, N_HEADS),
        in_specs=[
            col(0), col(N_HEADS), col(2 * N_HEADS),
            vec, vec, vec, vec,
            pl.BlockSpec((1, width), lambda b, h: (0, 0)),
        ],
        out_specs=pl.BlockSpec((seq, width), lambda b, h: (b, h)),
        out_shape=jax.ShapeDtypeStruct((t, N_HEADS * width), BF16),
        scratch_shapes=[pltpu.VMEM((width, seq), BF16),
                        pltpu.VMEM((2, seq, ATT_T), F32),
                        pltpu.VMEM((2, seq, ATT_T), F32),
                        pltpu.VMEM((seq, ATT_T), BF16),
                        pltpu.VMEM((seq, ATT_T), BF16),
                        pltpu.VMEM((width, ATT_T), F32)],
        compiler_params=_params(("parallel", "parallel"), 48),
        name="diff_attention",
    )(z, z, z, lq1, lk1, lq2, lk2, subln_g)


def _proj_residual_kernel(*refs, n_parts, n_cast):
    refs = list(refs)
    a_refs = [refs.pop(0) for _ in range(n_parts)]
    w_refs = [refs.pop(0) for _ in range(n_parts)]
    res_ref = refs.pop(0)
    cast_src = [refs.pop(0) for _ in range(n_cast)]
    o_ref = refs.pop(0)
    cast_dst = refs
    acc = res_ref[...]
    for a_ref, w_ref in zip(a_refs, w_refs):
        acc = acc + _dot(a_ref[...], w_ref[...])
    o_ref[...] = acc
    _run_cast_riders(cast_src, cast_dst)


def _proj_residual(parts, w, res, cast=()):
    t, n = res.shape
    tm = OUT_TM
    kp = parts[0].shape[1]
    n_parts = len(parts)
    in_specs = [pl.BlockSpec((tm, kp), lambda i: (i, 0)) for _ in parts]
    in_specs += [pl.BlockSpec((kp, n), functools.partial(lambda i, p: (p, 0), p=p))
                 for p in range(n_parts)]
    in_specs.append(pl.BlockSpec((tm, n), lambda i: (i, 0)))
    args = [*parts, *([w] * n_parts), res]
    out_specs = [pl.BlockSpec((tm, n), lambda i: (i, 0))]
    out_shape = [jax.ShapeDtypeStruct((t, n), F32)]
    for arr, lyr in cast:
        in_spec, out_spec, shape = _cast_rider(arr, lyr, t // tm, lambda i: i)
        in_specs.append(in_spec)
        out_specs.append(out_spec)
        out_shape.append(shape)
        args.append(arr)
    out = pl.pallas_call(
        functools.partial(_proj_residual_kernel, n_parts=n_parts, n_cast=len(cast)),
        grid=(t // tm,),
        in_specs=in_specs,
        out_specs=out_specs,
        out_shape=out_shape,
        compiler_params=_params(("parallel",), 48),
        name="proj_residual",
    )(*args)
    return out if len(out) > 1 else out[0]


def _mlp_kernel(h_ref, g_ref, w1_ref, w2_ref, *refs, with_final, n_cast):
    refs = list(refs)
    fg_ref = refs.pop(0) if with_final else None
    cast_src = [refs.pop(0) for _ in range(n_cast)]
    o_ref = refs.pop(0)
    cast_dst = [refs.pop(0) for _ in range(n_cast)]
    xn_ref, acc_ref = refs
    f = pl.program_id(1)

    @pl.when(f == 0)
    def _():
        xn_ref[...] = _rms_normalize(h_ref[...], g_ref[...]).astype(BF16)
        acc_ref[...] = jnp.zeros_like(acc_ref)

    a = jnp.maximum(_dot(xn_ref[...], w1_ref[...]), 0.0)
    acc_ref[...] += _dot((a * a).astype(BF16), w2_ref[...])
    _run_cast_riders(cast_src, cast_dst)

    @pl.when(f == pl.num_programs(1) - 1)
    def _():
        out = h_ref[...] + acc_ref[...]
        if with_final:
            out = _rms_normalize(out, fg_ref[...])
        o_ref[...] = out


def _mlp(h, g, w1, w2, final_g=None, cast=()):
    t, d = h.shape
    dff = w1.shape[1]
    tm, tf = MLP_TM, MLP_TF
    with_final = final_g is not None
    n_steps = dff // tf
    in_specs = [
        pl.BlockSpec((tm, d), lambda i, f: (i, 0)),
        pl.BlockSpec((1, d), lambda i, f: (0, 0)),
        pl.BlockSpec((d, tf), lambda i, f: (0, f)),
        pl.BlockSpec((tf, d), lambda i, f: (f, 0)),
    ]
    args = [h, g, w1, w2]
    if with_final:
        in_specs.append(pl.BlockSpec((1, d), lambda i, f: (0, 0)))
        args.append(final_g)
    out_specs = [pl.BlockSpec((tm, d), lambda i, f: (i, 0))]
    out_shape = [jax.ShapeDtypeStruct((t, d), F32)]
    for arr, lyr in cast:
        in_spec, out_spec, shape = _cast_rider(arr, lyr, (t // tm) * n_steps,
                                               lambda i, f: i * n_steps + f)
        in_specs.append(in_spec)
        out_specs.append(out_spec)
        out_shape.append(shape)
        args.append(arr)
    out = pl.pallas_call(
        functools.partial(_mlp_kernel, with_final=with_final, n_cast=len(cast)),
        grid=(t // tm, n_steps),
        in_specs=in_specs,
        out_specs=out_specs,
        out_shape=out_shape,
        scratch_shapes=[pltpu.VMEM((tm, d), BF16), pltpu.VMEM((tm, d), F32)],
        compiler_params=_params(("parallel", "arbitrary"), 56),
        name="mlp_final" if with_final else "mlp",
    )(*args)
    return out if len(out) > 1 else out[0]


def kernel(x, norm_mix_g, norm_mlp_g, even_w_in, even_b_f, even_ret_gn, even_w_out,
           odd_w_in, odd_lambda_q1, odd_lambda_k1, odd_lambda_q2, odd_lambda_k2,
           odd_subln_g, odd_w_out, mlp_w1, mlp_w2, final_g):
    batch, seq, d = x.shape
    t = batch * seq
    ret_w = N_HEADS * HEAD_DIM
    row = lambda v: v.reshape(1, -1).astype(F32)

    half = HEAD_DIM // 2
    inv = ROPE_BASE ** (-jnp.arange(half, dtype=F32) / half)
    ang = jnp.arange(seq, dtype=F32)[:, None] * inv[None, :]
    cos = jnp.concatenate([jnp.cos(ang), jnp.cos(ang)], axis=-1)
    sin_signed = jnp.concatenate([-jnp.sin(ang), jnp.sin(ang)], axis=-1)
    log_g = jnp.log1p(-(2.0 ** (-5.0 - jnp.arange(N_HEADS, dtype=F32))))

    h = x.reshape(t, d)

    n_main = 7 * ret_w
    wf = jnp.pad(even_w_in[0, :, n_main:], ((0, 0), (0, LANES - N_HEADS)))
    w_in_t = jnp.swapaxes(even_w_in, 1, 2).astype(BF16)
    z, ff, w1_0, w2_0, w_out_0 = _norm_proj(
        h, row(norm_mix_g[0]), w_in_t, 0, n_main, HEAD_DIM, wf, w_transposed=True,
        cast=[(mlp_w1, 0), (mlp_w2, 0), (even_w_out, 0)])
    b_f = jnp.pad(even_b_f[0].astype(F32), (0, LANES - N_HEADS)).reshape(1, LANES)
    fbias = _forget_bias(ff, b_f, batch, seq)
    ret = _retention(z, log_g, cos, sin_signed, row(even_ret_gn[0]), batch, seq)
    fox = _fox_attention(z, fbias, batch, seq)
    h = _proj_residual([ret, fox], w_out_0, h)
    h, w1_1, w2_1, w_in_1, w_out_1 = _mlp(
        h, row(norm_mlp_g[0]), w1_0, w2_0,
        cast=[(mlp_w1, 1), (mlp_w2, 1), (odd_w_in, 0), (odd_w_out, 0)])

    lambda_init = 0.8 - 0.6 * math.exp(-0.3 * 1)
    z = _norm_proj(h, row(norm_mix_g[1]), w_in_1[None], 0, w_in_1.shape[1], 2 * HEAD_DIM)
    att = _diff_attention(z, row(odd_lambda_q1[0]), row(odd_lambda_k1[0]),
                          row(odd_lambda_q2[0]), row(odd_lambda_k2[0]),
                          row(odd_subln_g[0]), lambda_init, batch, seq)
    h = _proj_residual([att], w_out_1, h)
    h = _mlp(h, row(norm_mlp_g[1]), w1_1, w2_1, final_g=row(final_g))
    return h.reshape(batch, seq, d)
```

```python
import functools
import math

import jax
import jax.numpy as jnp
from jax import lax
from jax.experimental import pallas as pl
from jax.experimental.pallas import tpu as pltpu

F32 = jnp.float32
BF16 = jnp.bfloat16

CHUNK = 64
CHUNK_SHIFT = 6
HEAD_DIM = 128
N_HEADS = 8
ROPE_BASE = 10000.0
RMS_EPS = 1e-6
GN_EPS = 1e-5
LOG2E = 1.4426950408889634
LANES = 128
SUBLANES = 8
BF16_ROWS = 16
MIB = 1024 * 1024

MXU_COLS = 256
PROJ_TM = 1024
PROJ_TN_MAX = 2048
OUT_TM = 512
MLP_TM = 512
MLP_TF = 1024
RET_ROWS = 256
ATT_T = 512
CUMSUM_ROWS = 256


def _params(semantics, vmem_mib):
    return pltpu.CompilerParams(dimension_semantics=semantics,
                                vmem_limit_bytes=vmem_mib * MIB)


def _dot(a, b):
    return jnp.dot(a, b, preferred_element_type=F32)


def _dot_nt(a, b):
    return lax.dot_general(a, b, (((1,), (1,)), ((), ())), preferred_element_type=F32)


def _dot_tn(a, b):
    return lax.dot_general(a, b, (((0,), (0,)), ((), ())), preferred_element_type=F32)


def _rms_normalize(x, g):
    ms = jnp.mean(x * x, axis=-1, keepdims=True)
    return (x * lax.rsqrt(ms + RMS_EPS)) * g


def _slab_reduce(x, op):
    rows, n = x.shape
    return op(x.reshape(rows // SUBLANES, SUBLANES, n), axis=0)


def _cast_rider(arr, layer, n_steps, step_index):
    rows, cols = arr.shape[1:]
    slab = max(BF16_ROWS, rows // n_steps)
    share = slab * n_steps // rows
    in_spec = pl.BlockSpec((None, slab, cols),
                           lambda *ids: (layer, step_index(*ids) // share, 0))
    out_spec = pl.BlockSpec((slab, cols), lambda *ids: (step_index(*ids) // share, 0))
    return in_spec, out_spec, jax.ShapeDtypeStruct((rows, cols), BF16), share


def _run_cast_riders(src_refs, dst_refs, shares=None, step=None):
    for k, (src, dst) in enumerate(zip(src_refs, dst_refs)):
        def cast(src=src, dst=dst):
            dst[...] = src[...].astype(dst.dtype)
        if shares is None or shares[k] == 1:
            cast()
        else:
            pl.when(lax.rem(step, shares[k]) == 0)(cast)


def _norm_proj_kernel(*refs, with_forget, w_transposed, n_cast):
    x_ref, g_ref, w_ref = refs[:3]
    refs = list(refs[3:])
    wf_ref = refs.pop(0) if with_forget else None
    cast_src = [refs.pop(0) for _ in range(n_cast)]
    o_ref = refs.pop(0)
    f_ref = refs.pop(0) if with_forget else None
    cast_dst = [refs.pop(0) for _ in range(n_cast)]
    (xn_ref,) = refs

    @pl.when(pl.program_id(1) == 0)
    def _():
        xn_ref[...] = _rms_normalize(x_ref[...], g_ref[...]).astype(BF16)
        if with_forget:
            f_ref[...] = _dot(xn_ref[...], wf_ref[...].astype(BF16))

    matmul = _dot_nt if w_transposed else _dot
    res = matmul(xn_ref[...], w_ref[...]).astype(o_ref.dtype)
    group = o_ref.shape[2]
    for c in range(o_ref.shape[0]):
        o_ref[c] = res[:, c * group:(c + 1) * group]
    _run_cast_riders(cast_src, cast_dst)


def _norm_proj(x, g, w, layer, n, group, wf=None, w_transposed=False, cast=()):
    t, d = x.shape
    tm = PROJ_TM
    tn = max(c for c in range(MXU_COLS, PROJ_TN_MAX + 1, MXU_COLS) if n % c == 0)
    with_forget = wf is not None
    n_col_steps = n // tn
    riders = [_cast_rider(arr, lyr, (t // tm) * n_col_steps, lambda i, j: i * n_col_steps + j)
              for arr, lyr in cast]
    if w_transposed:
        w_spec = pl.BlockSpec((None, tn, d), lambda i, j: (layer, j, 0))
    else:
        w_spec = pl.BlockSpec((None, d, tn), lambda i, j: (layer, 0, j))
    in_specs = [
        pl.BlockSpec((tm, d), lambda i, j: (i, 0)),
        pl.BlockSpec((1, d), lambda i, j: (0, 0)),
        w_spec,
    ]
    out_shape = [jax.ShapeDtypeStruct((n // group, t, group), BF16)]
    out_specs = [pl.BlockSpec((tn // group, tm, group), lambda i, j: (j, i, 0))]
    args = [x, g, w]
    if with_forget:
        in_specs.append(pl.BlockSpec((d, LANES), lambda i, j: (0, 0)))
        out_shape.append(jax.ShapeDtypeStruct((t, LANES), F32))
        out_specs.append(pl.BlockSpec((tm, LANES), lambda i, j: (i, 0)))
        args.append(wf)
    for (in_spec, out_spec, shape, _), (arr, _) in zip(riders, cast):
        in_specs.append(in_spec)
        out_specs.append(out_spec)
        out_shape.append(shape)
        args.append(arr)
    out = pl.pallas_call(
        functools.partial(_norm_proj_kernel, with_forget=with_forget,
                          w_transposed=w_transposed, n_cast=len(cast)),
        grid=(t // tm, n_col_steps),
        in_specs=in_specs,
        out_specs=out_specs,
        out_shape=out_shape,
        scratch_shapes=[pltpu.VMEM((tm, d), BF16)],
        compiler_params=_params(("parallel", "arbitrary"), 56),
        name="norm_proj_forget" if with_forget else "norm_proj",
    )(*args)
    return out if len(out) > 1 else out[0]


def _rotate(x, cos, sin_signed):
    return x * cos + pltpu.roll(x, HEAD_DIM // 2, 1) * sin_signed


def _retention_kernel(lg_ref, q_ref, k_ref, v_ref, gate_ref, cos_ref, sin_ref, gn_ref, o_ref):
    seq = q_ref.shape[0]
    rows = RET_ROWS
    lg = lg_ref[pl.program_id(1)]

    row = lax.broadcasted_iota(jnp.int32, (rows, HEAD_DIM), 0).astype(F32)
    xi = jnp.exp(lg * (row + 1.0))
    zeta = jnp.exp(lg * (rows - 1.0 - row))
    ti = lax.broadcasted_iota(jnp.int32, (rows, rows), 0)
    si = lax.broadcasted_iota(jnp.int32, (rows, rows), 1)
    visible = (si >> CHUNK_SHIFT) <= (ti >> CHUNK_SHIFT)
    decay = jnp.where(visible, jnp.exp(lg * jnp.abs(ti - si).astype(F32)), 0.0)
    block_decay = jnp.exp(jnp.full((1, HEAD_DIM), lg * rows, F32))
    gn = gn_ref[...]

    state = jnp.zeros((HEAD_DIM, HEAD_DIM), F32)
    for r in range(seq // rows):
        sl = slice(r * rows, (r + 1) * rows)
        cos = cos_ref[sl, :]
        sin = sin_ref[sl, :]
        q = _rotate(q_ref[sl, :].astype(F32), cos, sin) * (HEAD_DIM ** -0.5)
        k = _rotate(k_ref[sl, :].astype(F32), cos, sin)
        v = v_ref[sl, :]
        scores = _dot_nt(q.astype(BF16), k.astype(BF16)) * decay
        out = _dot(scores.astype(BF16), v) + _dot((q * xi).astype(BF16), state.astype(BF16))
        state = state * block_decay + _dot_tn((k * zeta).astype(BF16), v)

        ms = jnp.mean(out * out, axis=-1, keepdims=True)
        y = out * lax.rsqrt(ms + GN_EPS) * gn
        gate = gate_ref[sl, :].astype(F32)
        o_ref[sl, :] = (jax.nn.silu(gate) * y).astype(o_ref.dtype)


def _retention(z, log_g, cos, sin_signed, gn, batch, seq):
    t = z.shape[1]
    col = lambda off: pl.BlockSpec((None, seq, HEAD_DIM), lambda b, h: (off + h, b, 0))
    tab = pl.BlockSpec((seq, HEAD_DIM), lambda b, h: (0, 0))
    return pl.pallas_call(
        _retention_kernel,
        grid=(batch, N_HEADS),
        in_specs=[
            pl.BlockSpec(memory_space=pltpu.SMEM),
            col(0), col(N_HEADS), col(2 * N_HEADS), col(3 * N_HEADS),
            tab, tab,
            pl.BlockSpec((1, HEAD_DIM), lambda b, h: (0, h)),
        ],
        out_specs=pl.BlockSpec((seq, HEAD_DIM), lambda b, h: (b, h)),
        out_shape=jax.ShapeDtypeStruct((t, N_HEADS * HEAD_DIM), BF16),
        compiler_params=_params(("parallel", "parallel"), 32),
        name="retention",
    )(log_g, z, z, z, z, cos, sin_signed, gn)


def _forget_bias_kernel(ff_ref, b_ref, tri_ref, o_ref):
    seq = ff_ref.shape[0]
    rows = tri_ref.shape[0]
    lane = lax.broadcasted_iota(jnp.int32, (rows, LANES), 1)
    carry = jnp.zeros((1, LANES), F32)
    for blk in range(seq // rows):
        sl = slice(blk * rows, (blk + 1) * rows)
        log_f = jax.nn.log_sigmoid(ff_ref[sl, :] + b_ref[...])
        c = jnp.dot(tri_ref[...], log_f, precision=lax.Precision.HIGHEST,
                    preferred_element_type=F32) + carry
        carry = c[rows - 1:rows, :]
        for h in range(N_HEADS):
            x = jnp.broadcast_to(c[:, h:h + 1], (rows, LANES)) * (-LOG2E)
            hi = x.astype(BF16).astype(F32)
            mid = (x - hi).astype(BF16).astype(F32)
            lo = x - hi - mid
            pieces = jnp.where(lane == 0, hi, jnp.where(lane == 1, mid,
                                                       jnp.where(lane == 2, lo, 0.0)))
            o_ref[h, sl, :] = pieces.astype(o_ref.dtype)


def _forget_bias(ff, b_f, batch, seq):
    rows = CUMSUM_ROWS
    tri = (jnp.arange(rows)[None, :] <= jnp.arange(rows)[:, None]).astype(F32)
    out = pl.pallas_call(
        _forget_bias_kernel,
        grid=(batch,),
        in_specs=[
            pl.BlockSpec((seq, LANES), lambda b: (b, 0)),
            pl.BlockSpec((1, LANES), lambda b: (0, 0)),
            pl.BlockSpec((rows, rows), lambda b: (0, 0)),
        ],
        out_specs=pl.BlockSpec((None, N_HEADS, seq, LANES), lambda b: (b, 0, 0, 0)),
        out_shape=jax.ShapeDtypeStruct((batch, N_HEADS, seq, LANES), BF16),
        compiler_params=_params(("parallel",), 32),
        name="forget_bias",
    )(ff, b_f, tri)
    return out.reshape(batch * N_HEADS * seq, LANES)


def _pipeline_blocks(blocks):
    for task in blocks[0][0]:
        task()
    for n, (_, value_tasks) in enumerate(blocks):
        score_tasks = blocks[n + 1][0] if n + 1 < len(blocks) else []
        done = 0
        for idx, task in enumerate(value_tasks):
            task()
            while done < len(score_tasks) and done * len(value_tasks) < (idx + 1) * len(score_tasks):
                score_tasks[done]()
                done += 1


def _diag_tile_max(s, n_full, band_rows, band_mask):
    parts = []
    for j in range(s.shape[1] // LANES):
        lanes = slice(j * LANES, (j + 1) * LANES)
        nf = n_full(j)
        band = jnp.where(band_mask, s[nf:nf + band_rows, lanes], -jnp.inf)
        mx = _slab_reduce(band, jnp.max)
        if nf:
            mx = jnp.maximum(mx, _slab_reduce(s[:nf, lanes], jnp.max))
        parts.append(mx)
    return jnp.concatenate(parts, axis=1)


def _diag_tile_exp(s_ref, row0, m, n_full, band_rows, band_mask):
    t = s_ref.shape[1]
    strips, sums = [], []
    for j in range(t // LANES):
        lanes = slice(j * LANES, (j + 1) * LANES)
        nf = n_full(j)
        nv = nf + band_rows
        pj = jnp.exp2(s_ref[row0:row0 + nv, lanes] - m[:, lanes])
        band = jnp.where(band_mask, pj[nf:, :], 0.0)
        total = _slab_reduce(band, jnp.sum)
        pieces = [band.astype(BF16)]
        if nf:
            total = total + _slab_reduce(pj[:nf, :], jnp.sum)
            pieces.insert(0, pj[:nf, :].astype(BF16))
        if nv < t:
            pieces.append(jnp.zeros((t - nv, LANES), BF16))
        strips.append(jnp.concatenate(pieces, axis=0))
        sums.append(total)
    return jnp.concatenate(strips, axis=1), jnp.concatenate(sums, axis=1)


def _fox_kernel(q_ref, k_ref, v_ref, fb_ref, o_ref, vt_ref, s_ref):
    seq = q_ref.shape[0]
    t = ATT_T
    vt_ref[...] = v_ref[...].astype(F32).T.astype(BF16)
    lane = lax.broadcasted_iota(jnp.int32, (t, HEAD_DIM), 1)
    ones3 = jnp.where(lane < 3, 1.0, 0.0).astype(BF16)
    band_mask = (lax.broadcasted_iota(jnp.int32, (LANES, LANES), 0)
                 <= lax.broadcasted_iota(jnp.int32, (LANES, LANES), 1))
    n_full = lambda j: j * LANES
    qscale = (HEAD_DIM ** -0.5) * LOG2E

    def query_block(i):
        rows_i = slice(i * t, (i + 1) * t)
        sbuf = s_ref.at[i % 2]
        st = {"m_run": None, "l_run": jnp.zeros((SUBLANES, t), F32),
              "acc": jnp.zeros((HEAD_DIM, t), F32)}

        def score_task(c):
            rows_c = slice(c * t, (c + 1) * t)
            if c == 0:
                qs = (q_ref[rows_i, :].astype(F32) * qscale).astype(BF16)
                st["qa"] = jnp.concatenate([qs, ones3], axis=1)
            ka = jnp.concatenate([k_ref[rows_c, :], fb_ref[rows_c, :]], axis=1)
            s = _dot_nt(ka, st["qa"])
            sbuf[rows_c, :] = s
            if c == i:
                tile_max = _diag_tile_max(s, n_full, LANES, band_mask)
            else:
                tile_max = _slab_reduce(s, jnp.max)
            st["m_run"] = tile_max if c == 0 else jnp.maximum(st["m_run"], tile_max)

        def value_task(c):
            rows_c = slice(c * t, (c + 1) * t)
            if c == 0:
                st["m"] = jnp.max(st["m_run"], axis=0, keepdims=True)
            if c == i:
                p, sums = _diag_tile_exp(sbuf, c * t, st["m"], n_full, LANES, band_mask)
            else:
                pf = jnp.exp2(sbuf[rows_c, :] - st["m"])
                p, sums = pf.astype(BF16), _slab_reduce(pf, jnp.sum)
            st["l_run"] = st["l_run"] + sums
            st["acc"] = st["acc"] + _dot(vt_ref[:, rows_c], p)
            if c == i:
                l = jnp.sum(st["l_run"], axis=0, keepdims=True)
                o_ref[rows_i, :] = (st["acc"] / l).T.astype(o_ref.dtype)

        tiles = range(i + 1)
        return ([functools.partial(score_task, c) for c in tiles],
                [functools.partial(value_task, c) for c in tiles])

    _pipeline_blocks([query_block(i) for i in range(seq // t)])


def _fox_attention(z, fbias, batch, seq):
    t = z.shape[1]
    col = lambda off: pl.BlockSpec((None, seq, HEAD_DIM), lambda b, h: (off + h, b, 0))
    return pl.pallas_call(
        _fox_kernel,
        grid=(batch, N_HEADS),
        in_specs=[
            col(4 * N_HEADS), col(5 * N_HEADS), col(6 * N_HEADS),
            pl.BlockSpec((seq, LANES), lambda b, h: (b * N_HEADS + h, 0)),
        ],
        out_specs=pl.BlockSpec((seq, HEAD_DIM), lambda b, h: (b, h)),
        out_shape=jax.ShapeDtypeStruct((t, N_HEADS * HEAD_DIM), BF16),
        scratch_shapes=[pltpu.VMEM((HEAD_DIM, seq), BF16),
                        pltpu.VMEM((2, seq, ATT_T), F32)],
        compiler_params=_params(("parallel", "parallel"), 40),
        name="fox_attention",
    )(z, z, z, fbias)


def _diff_kernel(q_ref, k_ref, v_ref, lq1_ref, lk1_ref, lq2_ref, lk2_ref, g_ref, o_ref,
                 vt_ref, s1_ref, s2_ref, p1_ref, p2_ref, acc_ref, *, lambda_init):
    seq = q_ref.shape[0]
    t = ATT_T
    vt_ref[...] = v_ref[...].astype(F32).T.astype(BF16)
    lam = (jnp.exp(jnp.sum(lq1_ref[...] * lk1_ref[...], axis=-1, keepdims=True))
           - jnp.exp(jnp.sum(lq2_ref[...] * lk2_ref[...], axis=-1, keepdims=True))
           + lambda_init)
    band_mask = lax.broadcasted_iota(jnp.int32, (CHUNK, LANES), 1) >= CHUNK
    n_full = lambda j: j * LANES + CHUNK
    qscale = (HEAD_DIM ** -0.5) * LOG2E
    col_slices = (slice(0, HEAD_DIM), slice(HEAD_DIM, 2 * HEAD_DIM))
    def query_block(i):
        rows_i = slice(i * t, (i + 1) * t)
        s_refs = (s1_ref.at[i % 2], s2_ref.at[i % 2])
        p_refs = (p1_ref, p2_ref)
        zero = jnp.zeros((SUBLANES, t), F32)
        st = {"m_run": [None, None], "m": [None, None], "l_run": [zero, zero]}

        def score_task(mi, c):
            rows_c = slice(c * t, (c + 1) * t)
            if mi == 0 and c == 0:
                st["qs"] = (q_ref[rows_i, :].astype(F32) * qscale).astype(BF16)
            cols = col_slices[mi]
            s = _dot_nt(k_ref[rows_c, cols], st["qs"][:, cols])
            s_refs[mi][rows_c, :] = s
            if c == i:
                tile_max = _diag_tile_max(s, n_full, CHUNK, band_mask)
            else:
                tile_max = _slab_reduce(s, jnp.max)
            st["m_run"][mi] = tile_max if c == 0 else jnp.maximum(st["m_run"][mi], tile_max)

        def exp_task(mi, c):
            rows_c = slice(c * t, (c + 1) * t)
            if c == 0:
                st["m"][mi] = jnp.max(st["m_run"][mi], axis=0, keepdims=True)
            if c == i:
                p, sums = _diag_tile_exp(s_refs[mi], c * t, st["m"][mi], n_full, CHUNK,
                                         band_mask)
            else:
                pf = jnp.exp2(s_refs[mi][rows_c, :] - st["m"][mi])
                p, sums = pf.astype(BF16), _slab_reduce(pf, jnp.sum)
            p_refs[mi][rows_c, :] = p
            st["l_run"][mi] = st["l_run"][mi] + sums

        def value_task(c):
            rows_c = slice(c * t, (c + 1) * t)
            if c == 0:
                st["l1"] = jnp.sum(st["l_run"][0], axis=0, keepdims=True)
                l2 = jnp.sum(st["l_run"][1], axis=0, keepdims=True)
                st["rho"] = (lam * st["l1"] / l2).astype(BF16)
            a = p_refs[0][rows_c, :] - p_refs[1][rows_c, :] * st["rho"]
            av = _dot(vt_ref[:, rows_c], a)
            acc_ref[...] = av if c == 0 else acc_ref[...] + av
            if c == i:
                o = (acc_ref[...] / st["l1"]).T
                ms = jnp.mean(o * o, axis=-1, keepdims=True)
                y = o * lax.rsqrt(ms + GN_EPS) * g_ref[...] * (1.0 - lambda_init)
                o_ref[rows_i, :] = y.astype(o_ref.dtype)

        tiles = range(i + 1)
        return ([functools.partial(score_task, mi, c) for mi in (0, 1) for c in tiles],
                [functools.partial(exp_task, mi, c) for mi in (0, 1) for c in tiles]
                + [functools.partial(value_task, c) for c in tiles])

    _pipeline_blocks([query_block(i) for i in range(seq // t)])


def _diff_attention(z, lq1, lk1, lq2, lk2, subln_g, lambda_init, batch, seq):
    t = z.shape[1]
    width = 2 * HEAD_DIM
    col = lambda off: pl.BlockSpec((None, seq, width), lambda b, h: (off + h, b, 0))
    vec = pl.BlockSpec((1, HEAD_DIM), lambda b, h: (0, 0))
    return pl.pallas_call(
        functools.partial(_diff_kernel, lambda_init=lambda_init),
        grid=(batch, N_HEADS),
        in_specs=[
            col(0), col(N_HEADS), col(2 * N_HEADS),
            vec, vec, vec, vec,
            pl.BlockSpec((1, width), lambda b, h: (0, 0)),
        ],
        out_specs=pl.BlockSpec((seq, width), lambda b, h: (b, h)),
        out_shape=jax.ShapeDtypeStruct((t, N_HEADS * width), BF16),
        scratch_shapes=[pltpu.VMEM((width, seq), BF16),
                        pltpu.VMEM((2, seq, ATT_T), F32),
                        pltpu.VMEM((2, seq, ATT_T), F32),
                        pltpu.VMEM((seq, ATT_T), BF16),
                        pltpu.VMEM((seq, ATT_T), BF16),
                        pltpu.VMEM((width, ATT_T), F32)],
        compiler_params=_params(("parallel", "parallel"), 48),
        name="diff_attention",
    )(z, z, z, lq1, lk1, lq2, lk2, subln_g)


def _proj_residual_kernel(*refs, n_parts, n_cast):
    refs = list(refs)
    a_refs = [refs.pop(0) for _ in range(n_parts)]
    w_refs = [refs.pop(0) for _ in range(n_parts)]
    res_ref = refs.pop(0)
    cast_src = [refs.pop(0) for _ in range(n_cast)]
    o_ref = refs.pop(0)
    cast_dst = refs
    acc = res_ref[...]
    for a_ref, w_ref in zip(a_refs, w_refs):
        acc = acc + _dot(a_ref[...], w_ref[...])
    o_ref[...] = acc
    _run_cast_riders(cast_src, cast_dst)


def _proj_residual(parts, w, res, cast=()):
    t, n = res.shape
    tm = OUT_TM
    kp = parts[0].shape[1]
    n_parts = len(parts)
    in_specs = [pl.BlockSpec((tm, kp), lambda i: (i, 0)) for _ in parts]
    in_specs += [pl.BlockSpec((kp, n), functools.partial(lambda i, p: (p, 0), p=p))
                 for p in range(n_parts)]
    in_specs.append(pl.BlockSpec((tm, n), lambda i: (i, 0)))
    args = [*parts, *([w] * n_parts), res]
    out_specs = [pl.BlockSpec((tm, n), lambda i: (i, 0))]
    out_shape = [jax.ShapeDtypeStruct((t, n), F32)]
    for arr, lyr in cast:
        in_spec, out_spec, shape, _ = _cast_rider(arr, lyr, t // tm, lambda i: i)
        in_specs.append(in_spec)
        out_specs.append(out_spec)
        out_shape.append(shape)
        args.append(arr)
    out = pl.pallas_call(
        functools.partial(_proj_residual_kernel, n_parts=n_parts, n_cast=len(cast)),
        grid=(t // tm,),
        in_specs=in_specs,
        out_specs=out_specs,
        out_shape=out_shape,
        compiler_params=_params(("parallel",), 48),
        name="proj_residual",
    )(*args)
    return out if len(out) > 1 else out[0]


def _mlp_kernel(h_ref, g_ref, w1_ref, w2_ref, *refs, with_final, cast_shares, n_steps):
    refs = list(refs)
    n_cast = len(cast_shares)
    fg_ref = refs.pop(0) if with_final else None
    cast_src = [refs.pop(0) for _ in range(n_cast)]
    o_ref = refs.pop(0)
    cast_dst = [refs.pop(0) for _ in range(n_cast)]
    xn_ref, acc_ref = refs
    f = pl.program_id(1)

    @pl.when(f == 0)
    def _():
        xn_ref[...] = _rms_normalize(h_ref[...], g_ref[...]).astype(BF16)
        acc_ref[...] = jnp.zeros_like(acc_ref)

    a = jnp.maximum(_dot(xn_ref[...], w1_ref[...]), 0.0)
    acc_ref[...] += _dot((a * a).astype(BF16), w2_ref[...])
    _run_cast_riders(cast_src, cast_dst, cast_shares, pl.program_id(0) * n_steps + f)

    @pl.when(f == pl.num_programs(1) - 1)
    def _():
        out = h_ref[...] + acc_ref[...]
        if with_final:
            out = _rms_normalize(out, fg_ref[...])
        o_ref[...] = out


def _mlp(h, g, w1, w2, final_g=None, cast=()):
    t, d = h.shape
    dff = w1.shape[1]
    tm, tf = MLP_TM, MLP_TF
    with_final = final_g is not None
    n_steps = dff // tf
    in_specs = [
        pl.BlockSpec((tm, d), lambda i, f: (i, 0)),
        pl.BlockSpec((1, d), lambda i, f: (0, 0)),
        pl.BlockSpec((d, tf), lambda i, f: (0, f)),
        pl.BlockSpec((tf, d), lambda i, f: (f, 0)),
    ]
    args = [h, g, w1, w2]
    if with_final:
        in_specs.append(pl.BlockSpec((1, d), lambda i, f: (0, 0)))
        args.append(final_g)
    out_specs = [pl.BlockSpec((tm, d), lambda i, f: (i, 0))]
    out_shape = [jax.ShapeDtypeStruct((t, d), F32)]
    shares = []
    for arr, lyr in cast:
        in_spec, out_spec, shape, share = _cast_rider(arr, lyr, (t // tm) * n_steps,
                                                      lambda i, f: i * n_steps + f)
        in_specs.append(in_spec)
        out_specs.append(out_spec)
        out_shape.append(shape)
        args.append(arr)
        shares.append(share)
    out = pl.pallas_call(
        functools.partial(_mlp_kernel, with_final=with_final, cast_shares=tuple(shares),
                          n_steps=n_steps),
        grid=(t // tm, n_steps),
        in_specs=in_specs,
        out_specs=out_specs,
        out_shape=out_shape,
        scratch_shapes=[pltpu.VMEM((tm, d), BF16), pltpu.VMEM((tm, d), F32)],
        compiler_params=_params(("parallel", "arbitrary"), 56),
        name="mlp_final" if with_final else "mlp",
    )(*args)
    return out if len(out) > 1 else out[0]


def kernel(x, norm_mix_g, norm_mlp_g, even_w_in, even_b_f, even_ret_gn, even_w_out,
           odd_w_in, odd_lambda_q1, odd_lambda_k1, odd_lambda_q2, odd_lambda_k2,
           odd_subln_g, odd_w_out, mlp_w1, mlp_w2, final_g):
    batch, seq, d = x.shape
    t = batch * seq
    ret_w = N_HEADS * HEAD_DIM
    row = lambda v: v.reshape(1, -1).astype(F32)

    half = HEAD_DIM // 2
    inv = ROPE_BASE ** (-jnp.arange(half, dtype=F32) / half)
    ang = jnp.arange(seq, dtype=F32)[:, None] * inv[None, :]
    cos = jnp.concatenate([jnp.cos(ang), jnp.cos(ang)], axis=-1)
    sin_signed = jnp.concatenate([-jnp.sin(ang), jnp.sin(ang)], axis=-1)
    log_g = jnp.log1p(-(2.0 ** (-5.0 - jnp.arange(N_HEADS, dtype=F32))))

    h = x.reshape(t, d)

    n_main = 7 * ret_w
    wf = jnp.pad(even_w_in[0, :, n_main:], ((0, 0), (0, LANES - N_HEADS)))
    w_in_t = jnp.swapaxes(even_w_in, 1, 2).astype(BF16)
    z, ff, w1_0, w2_0, w_out_0 = _norm_proj(
        h, row(norm_mix_g[0]), w_in_t, 0, n_main, HEAD_DIM, wf, w_transposed=True,
        cast=[(mlp_w1, 0), (mlp_w2, 0), (even_w_out, 0)])
    b_f = jnp.pad(even_b_f[0].astype(F32), (0, LANES - N_HEADS)).reshape(1, LANES)
    fbias = _forget_bias(ff, b_f, batch, seq)
    ret = _retention(z, log_g, cos, sin_signed, row(even_ret_gn[0]), batch, seq)
    fox = _fox_attention(z, fbias, batch, seq)
    h = _proj_residual([ret, fox], w_out_0, h)
    h, w1_1, w2_1, w_in_1, w_out_1 = _mlp(
        h, row(norm_mlp_g[0]), w1_0, w2_0,
        cast=[(mlp_w1, 1), (mlp_w2, 1), (odd_w_in, 0), (odd_w_out, 0)])

    lambda_init = 0.8 - 0.6 * math.exp(-0.3 * 1)
    z = _norm_proj(h, row(norm_mix_g[1]), w_in_1[None], 0, w_in_1.shape[1], 2 * HEAD_DIM)
    att = _diff_attention(z, row(odd_lambda_q1[0]), row(odd_lambda_k1[0]),
                          row(odd_lambda_q2[0]), row(odd_lambda_k2[0]),
                          row(odd_subln_g[0]), lambda_init, batch, seq)
    h = _proj_residual([att], w_out_1, h)
    h = _mlp(h, row(norm_mlp_g[1]), w1_1, w2_1, final_g=row(final_g))
    return h.reshape(batch, seq, d)
```

```python
import functools
import math

import jax
import jax.numpy as jnp
from jax import lax
from jax.experimental import pallas as pl
from jax.experimental.pallas import tpu as pltpu

F32 = jnp.float32
BF16 = jnp.bfloat16

CHUNK = 64
CHUNK_SHIFT = 6
HEAD_DIM = 128
N_HEADS = 8
ROPE_BASE = 10000.0
RMS_EPS = 1e-6
GN_EPS = 1e-5
LOG2E = 1.4426950408889634
LANES = 128
SUBLANES = 8
BF16_ROWS = 16
MIB = 1024 * 1024

MXU_COLS = 256
PROJ_TM = 1024
PROJ_TN_MAX = 2048
OUT_TM = 512
MLP_TM = 512
MLP_TF = 1024
RET_ROWS = 256
ATT_T = 512
CUMSUM_ROWS = 256


def _params(semantics, vmem_mib):
    return pltpu.CompilerParams(dimension_semantics=semantics,
                                vmem_limit_bytes=vmem_mib * MIB)


def _dot(a, b):
    return jnp.dot(a, b, preferred_element_type=F32)


def _dot_nt(a, b):
    return lax.dot_general(a, b, (((1,), (1,)), ((), ())), preferred_element_type=F32)


def _dot_tn(a, b):
    return lax.dot_general(a, b, (((0,), (0,)), ((), ())), preferred_element_type=F32)


def _rms_normalize(x, g):
    ms = jnp.mean(x * x, axis=-1, keepdims=True)
    return (x * lax.rsqrt(ms + RMS_EPS)) * g


def _slab_reduce(x, op):
    rows, n = x.shape
    return op(x.reshape(rows // SUBLANES, SUBLANES, n), axis=0)


def _cast_rider(arr, layer, n_steps, step_index):
    rows, cols = arr.shape[1:]
    slab = max(BF16_ROWS, rows // n_steps)
    share = slab * n_steps // rows
    in_spec = pl.BlockSpec((None, slab, cols),
                           lambda *ids: (layer, step_index(*ids) // share, 0))
    out_spec = pl.BlockSpec((slab, cols), lambda *ids: (step_index(*ids) // share, 0))
    return in_spec, out_spec, jax.ShapeDtypeStruct((rows, cols), BF16), share


def _run_cast_riders(src_refs, dst_refs, shares=None, step=None):
    for k, (src, dst) in enumerate(zip(src_refs, dst_refs)):
        def cast(src=src, dst=dst):
            dst[...] = src[...].astype(dst.dtype)
        if shares is None or shares[k] == 1:
            cast()
        else:
            pl.when(lax.rem(step, shares[k]) == 0)(cast)


def _norm_proj_kernel(*refs, with_forget, w_transposed, n_cast):
    x_ref, g_ref, w_ref = refs[:3]
    refs = list(refs[3:])
    wf_ref = refs.pop(0) if with_forget else None
    cast_src = [refs.pop(0) for _ in range(n_cast)]
    o_ref = refs.pop(0)
    f_ref = refs.pop(0) if with_forget else None
    cast_dst = [refs.pop(0) for _ in range(n_cast)]
    (xn_ref,) = refs

    @pl.when(pl.program_id(1) == 0)
    def _():
        xn_ref[...] = _rms_normalize(x_ref[...], g_ref[...]).astype(BF16)
        if with_forget:
            f_ref[...] = _dot(xn_ref[...], wf_ref[...].astype(BF16))

    matmul = _dot_nt if w_transposed else _dot
    res = matmul(xn_ref[...], w_ref[...]).astype(o_ref.dtype)
    group = o_ref.shape[2]
    for c in range(o_ref.shape[0]):
        o_ref[c] = res[:, c * group:(c + 1) * group]
    _run_cast_riders(cast_src, cast_dst)


def _norm_proj(x, g, w, layer, n, group, wf=None, w_transposed=False, cast=()):
    t, d = x.shape
    tm = PROJ_TM
    tn = max(c for c in range(MXU_COLS, PROJ_TN_MAX + 1, MXU_COLS) if n % c == 0)
    with_forget = wf is not None
    n_col_steps = n // tn
    riders = [_cast_rider(arr, lyr, (t // tm) * n_col_steps, lambda i, j: i * n_col_steps + j)
              for arr, lyr in cast]
    if w_transposed:
        w_spec = pl.BlockSpec((None, tn, d), lambda i, j: (layer, j, 0))
    else:
        w_spec = pl.BlockSpec((None, d, tn), lambda i, j: (layer, 0, j))
    in_specs = [
        pl.BlockSpec((tm, d), lambda i, j: (i, 0)),
        pl.BlockSpec((1, d), lambda i, j: (0, 0)),
        w_spec,
    ]
    out_shape = [jax.ShapeDtypeStruct((n // group, t, group), BF16)]
    out_specs = [pl.BlockSpec((tn // group, tm, group), lambda i, j: (j, i, 0))]
    args = [x, g, w]
    if with_forget:
        in_specs.append(pl.BlockSpec((d, LANES), lambda i, j: (0, 0)))
        out_shape.append(jax.ShapeDtypeStruct((t, LANES), F32))
        out_specs.append(pl.BlockSpec((tm, LANES), lambda i, j: (i, 0)))
        args.append(wf)
    for (in_spec, out_spec, shape, _), (arr, _) in zip(riders, cast):
        in_specs.append(in_spec)
        out_specs.append(out_spec)
        out_shape.append(shape)
        args.append(arr)
    out = pl.pallas_call(
        functools.partial(_norm_proj_kernel, with_forget=with_forget,
                          w_transposed=w_transposed, n_cast=len(cast)),
        grid=(t // tm, n_col_steps),
        in_specs=in_specs,
        out_specs=out_specs,
        out_shape=out_shape,
        scratch_shapes=[pltpu.VMEM((tm, d), BF16)],
        compiler_params=_params(("parallel", "arbitrary"), 56),
        name="norm_proj_forget" if with_forget else "norm_proj",
    )(*args)
    return out if len(out) > 1 else out[0]


def _rotate(x, cos, sin_signed):
    return x * cos + pltpu.roll(x, HEAD_DIM // 2, 1) * sin_signed


def _retention_kernel(lg_ref, q_ref, k_ref, v_ref, gate_ref, cos_ref, sin_ref, gn_ref, o_ref):
    seq = q_ref.shape[0]
    rows = RET_ROWS
    lg = lg_ref[pl.program_id(1)]

    row = lax.broadcasted_iota(jnp.int32, (rows, HEAD_DIM), 0).astype(F32)
    xi = jnp.exp(lg * (row + 1.0))
    zeta = jnp.exp(lg * (rows - 1.0 - row))
    ti = lax.broadcasted_iota(jnp.int32, (rows, rows), 0)
    si = lax.broadcasted_iota(jnp.int32, (rows, rows), 1)
    visible = (si >> CHUNK_SHIFT) <= (ti >> CHUNK_SHIFT)
    decay = jnp.where(visible, jnp.exp(lg * jnp.abs(ti - si).astype(F32)), 0.0)
    block_decay = jnp.exp(jnp.full((1, HEAD_DIM), lg * rows, F32))
    gn = gn_ref[...]

    state = jnp.zeros((HEAD_DIM, HEAD_DIM), F32)
    for r in range(seq // rows):
        sl = slice(r * rows, (r + 1) * rows)
        cos = cos_ref[sl, :]
        sin = sin_ref[sl, :]
        q = _rotate(q_ref[sl, :].astype(F32), cos, sin) * (HEAD_DIM ** -0.5)
        k = _rotate(k_ref[sl, :].astype(F32), cos, sin)
        v = v_ref[sl, :]
        scores = _dot_nt(q.astype(BF16), k.astype(BF16)) * decay
        out = _dot(scores.astype(BF16), v) + _dot((q * xi).astype(BF16), state.astype(BF16))
        state = state * block_decay + _dot_tn((k * zeta).astype(BF16), v)

        ms = jnp.mean(out * out, axis=-1, keepdims=True)
        y = out * lax.rsqrt(ms + GN_EPS) * gn
        gate = gate_ref[sl, :].astype(F32)
        o_ref[sl, :] = (jax.nn.silu(gate) * y).astype(o_ref.dtype)


def _retention(z, log_g, cos, sin_signed, gn, batch, seq):
    t = z.shape[1]
    col = lambda off: pl.BlockSpec((None, seq, HEAD_DIM), lambda b, h: (off + h, b, 0))
    tab = pl.BlockSpec((seq, HEAD_DIM), lambda b, h: (0, 0))
    return pl.pallas_call(
        _retention_kernel,
        grid=(batch, N_HEADS),
        in_specs=[
            pl.BlockSpec(memory_space=pltpu.SMEM),
            col(0), col(N_HEADS), col(2 * N_HEADS), col(3 * N_HEADS),
            tab, tab,
            pl.BlockSpec((1, HEAD_DIM), lambda b, h: (0, h)),
        ],
        out_specs=pl.BlockSpec((seq, HEAD_DIM), lambda b, h: (b, h)),
        out_shape=jax.ShapeDtypeStruct((t, N_HEADS * HEAD_DIM), BF16),
        compiler_params=_params(("parallel", "parallel"), 32),
        name="retention",
    )(log_g, z, z, z, z, cos, sin_signed, gn)


def _forget_bias_kernel(ff_ref, b_ref, tri_ref, o_ref):
    seq = ff_ref.shape[0]
    rows = tri_ref.shape[0]
    lane = lax.broadcasted_iota(jnp.int32, (rows, LANES), 1)
    carry = jnp.zeros((1, LANES), F32)
    for blk in range(seq // rows):
        sl = slice(blk * rows, (blk + 1) * rows)
        log_f = jax.nn.log_sigmoid(ff_ref[sl, :] + b_ref[...])
        c = jnp.dot(tri_ref[...], log_f, precision=lax.Precision.HIGHEST,
                    preferred_element_type=F32) + carry
        carry = c[rows - 1:rows, :]
        for h in range(N_HEADS):
            x = jnp.broadcast_to(c[:, h:h + 1], (rows, LANES)) * (-LOG2E)
            hi = x.astype(BF16).astype(F32)
            mid = (x - hi).astype(BF16).astype(F32)
            lo = x - hi - mid
            pieces = jnp.where(lane == 0, hi, jnp.where(lane == 1, mid,
                                                       jnp.where(lane == 2, lo, 0.0)))
            o_ref[h, sl, :] = pieces.astype(o_ref.dtype)


def _forget_bias(ff, b_f, batch, seq):
    rows = CUMSUM_ROWS
    tri = (jnp.arange(rows)[None, :] <= jnp.arange(rows)[:, None]).astype(F32)
    out = pl.pallas_call(
        _forget_bias_kernel,
        grid=(batch,),
        in_specs=[
            pl.BlockSpec((seq, LANES), lambda b: (b, 0)),
            pl.BlockSpec((1, LANES), lambda b: (0, 0)),
            pl.BlockSpec((rows, rows), lambda b: (0, 0)),
        ],
        out_specs=pl.BlockSpec((None, N_HEADS, seq, LANES), lambda b: (b, 0, 0, 0)),
        out_shape=jax.ShapeDtypeStruct((batch, N_HEADS, seq, LANES), BF16),
        compiler_params=_params(("parallel",), 32),
        name="forget_bias",
    )(ff, b_f, tri)
    return out.reshape(batch * N_HEADS * seq, LANES)


def _pipeline_blocks(blocks):
    for task in blocks[0][0]:
        task()
    for n, (_, value_tasks) in enumerate(blocks):
        score_tasks = blocks[n + 1][0] if n + 1 < len(blocks) else []
        done = 0
        for idx, task in enumerate(value_tasks):
            task()
            while done < len(score_tasks) and done * len(value_tasks) < (idx + 1) * len(score_tasks):
                score_tasks[done]()
                done += 1


def _diag_tile_max(s, n_full, band_rows, band_mask):
    parts = []
    for j in range(s.shape[1] // LANES):
        lanes = slice(j * LANES, (j + 1) * LANES)
        nf = n_full(j)
        band = jnp.where(band_mask, s[nf:nf + band_rows, lanes], -jnp.inf)
        mx = _slab_reduce(band, jnp.max)
        if nf:
            mx = jnp.maximum(mx, _slab_reduce(s[:nf, lanes], jnp.max))
        parts.append(mx)
    return jnp.concatenate(parts, axis=1)


def _diag_tile_exp(s_ref, row0, m, n_full, band_rows, band_mask):
    t = s_ref.shape[1]
    strips, sums = [], []
    for j in range(t // LANES):
        lanes = slice(j * LANES, (j + 1) * LANES)
        nf = n_full(j)
        nv = nf + band_rows
        pj = jnp.exp2(s_ref[row0:row0 + nv, lanes] - m[:, lanes])
        band = jnp.where(band_mask, pj[nf:, :], 0.0)
        total = _slab_reduce(band, jnp.sum)
        pieces = [band.astype(BF16)]
        if nf:
            total = total + _slab_reduce(pj[:nf, :], jnp.sum)
            pieces.insert(0, pj[:nf, :].astype(BF16))
        if nv < t:
            pieces.append(jnp.zeros((t - nv, LANES), BF16))
        strips.append(jnp.concatenate(pieces, axis=0))
        sums.append(total)
    return jnp.concatenate(strips, axis=1), jnp.concatenate(sums, axis=1)


def _fox_kernel(q_ref, k_ref, v_ref, fb_ref, *refs, n_cast):
    refs = list(refs)
    cast_src = [refs.pop(0) for _ in range(n_cast)]
    o_ref = refs.pop(0)
    cast_dst = [refs.pop(0) for _ in range(n_cast)]
    vt_ref, s_ref = refs
    seq = q_ref.shape[0]
    t = ATT_T
    _run_cast_riders(cast_src, cast_dst)
    vt_ref[...] = v_ref[...].astype(F32).T.astype(BF16)
    lane = lax.broadcasted_iota(jnp.int32, (t, HEAD_DIM), 1)
    ones3 = jnp.where(lane < 3, 1.0, 0.0).astype(BF16)
    band_mask = (lax.broadcasted_iota(jnp.int32, (LANES, LANES), 0)
                 <= lax.broadcasted_iota(jnp.int32, (LANES, LANES), 1))
    n_full = lambda j: j * LANES
    qscale = (HEAD_DIM ** -0.5) * LOG2E

    def query_block(i):
        rows_i = slice(i * t, (i + 1) * t)
        sbuf = s_ref.at[i % 2]
        st = {"m_run": None, "l_run": jnp.zeros((SUBLANES, t), F32),
              "acc": jnp.zeros((HEAD_DIM, t), F32)}

        def score_task(c):
            rows_c = slice(c * t, (c + 1) * t)
            if c == 0:
                qs = (q_ref[rows_i, :].astype(F32) * qscale).astype(BF16)
                st["qa"] = jnp.concatenate([qs, ones3], axis=1)
            ka = jnp.concatenate([k_ref[rows_c, :], fb_ref[rows_c, :]], axis=1)
            s = _dot_nt(ka, st["qa"])
            sbuf[rows_c, :] = s
            if c == i:
                tile_max = _diag_tile_max(s, n_full, LANES, band_mask)
            else:
                tile_max = _slab_reduce(s, jnp.max)
            st["m_run"] = tile_max if c == 0 else jnp.maximum(st["m_run"], tile_max)

        def value_task(c):
            rows_c = slice(c * t, (c + 1) * t)
            if c == 0:
                st["m"] = jnp.max(st["m_run"], axis=0, keepdims=True)
            if c == i:
                p, sums = _diag_tile_exp(sbuf, c * t, st["m"], n_full, LANES, band_mask)
            else:
                pf = jnp.exp2(sbuf[rows_c, :] - st["m"])
                p, sums = pf.astype(BF16), _slab_reduce(pf, jnp.sum)
            st["l_run"] = st["l_run"] + sums
            st["acc"] = st["acc"] + _dot(vt_ref[:, rows_c], p)
            if c == i:
                l = jnp.sum(st["l_run"], axis=0, keepdims=True)
                o_ref[rows_i, :] = (st["acc"] / l).T.astype(o_ref.dtype)

        tiles = range(i + 1)
        return ([functools.partial(score_task, c) for c in tiles],
                [functools.partial(value_task, c) for c in tiles])

    _pipeline_blocks([query_block(i) for i in range(seq // t)])


def _fox_attention(z, fbias, batch, seq, cast=()):
    t = z.shape[1]
    col = lambda off: pl.BlockSpec((None, seq, HEAD_DIM), lambda b, h: (off + h, b, 0))
    in_specs = [
        col(4 * N_HEADS), col(5 * N_HEADS), col(6 * N_HEADS),
        pl.BlockSpec((seq, LANES), lambda b, h: (b * N_HEADS + h, 0)),
    ]
    args = [z, z, z, fbias]
    out_specs = [pl.BlockSpec((seq, HEAD_DIM), lambda b, h: (b, h))]
    out_shape = [jax.ShapeDtypeStruct((t, N_HEADS * HEAD_DIM), BF16)]
    for arr, lyr in cast:
        in_spec, out_spec, shape, _ = _cast_rider(arr, lyr, batch * N_HEADS,
                                                  lambda b, h: b * N_HEADS + h)
        in_specs.append(in_spec)
        out_specs.append(out_spec)
        out_shape.append(shape)
        args.append(arr)
    out = pl.pallas_call(
        functools.partial(_fox_kernel, n_cast=len(cast)),
        grid=(batch, N_HEADS),
        in_specs=in_specs,
        out_specs=out_specs,
        out_shape=out_shape,
        scratch_shapes=[pltpu.VMEM((HEAD_DIM, seq), BF16),
                        pltpu.VMEM((2, seq, ATT_T), F32)],
        compiler_params=_params(("parallel", "parallel"), 48),
        name="fox_attention",
    )(*args)
    return out if len(out) > 1 else out[0]


def _diff_kernel(q_ref, k_ref, v_ref, lq1_ref, lk1_ref, lq2_ref, lk2_ref, g_ref, o_ref,
                 vt_ref, s1_ref, s2_ref, p1_ref, p2_ref, acc_ref, *, lambda_init):
    seq = q_ref.shape[0]
    t = ATT_T
    vt_ref[...] = v_ref[...].astype(F32).T.astype(BF16)
    lam = (jnp.exp(jnp.sum(lq1_ref[...] * lk1_ref[...], axis=-1, keepdims=True))
           - jnp.exp(jnp.sum(lq2_ref[...] * lk2_ref[...], axis=-1, keepdims=True))
           + lambda_init)
    band_mask = lax.broadcasted_iota(jnp.int32, (CHUNK, LANES), 1) >= CHUNK
    n_full = lambda j: j * LANES + CHUNK
    qscale = (HEAD_DIM ** -0.5) * LOG2E
    col_slices = (slice(0, HEAD_DIM), slice(HEAD_DIM, 2 * HEAD_DIM))
    def query_block(i):
        rows_i = slice(i * t, (i + 1) * t)
        s_refs = (s1_ref.at[i % 2], s2_ref.at[i % 2])
        p_refs = (p1_ref, p2_ref)
        zero = jnp.zeros((SUBLANES, t), F32)
        st = {"m_run": [None, None], "m": [None, None], "l_run": [zero, zero]}

        def score_task(mi, c):
            rows_c = slice(c * t, (c + 1) * t)
            if mi == 0 and c == 0:
                st["qs"] = (q_ref[rows_i, :].astype(F32) * qscale).astype(BF16)
            cols = col_slices[mi]
            s = _dot_nt(k_ref[rows_c, cols], st["qs"][:, cols])
            s_refs[mi][rows_c, :] = s
            if c == i:
                tile_max = _diag_tile_max(s, n_full, CHUNK, band_mask)
            else:
                tile_max = _slab_reduce(s, jnp.max)
            st["m_run"][mi] = tile_max if c == 0 else jnp.maximum(st["m_run"][mi], tile_max)

        def exp_task(mi, c):
            rows_c = slice(c * t, (c + 1) * t)
            if c == 0:
                st["m"][mi] = jnp.max(st["m_run"][mi], axis=0, keepdims=True)
            if c == i:
                p, sums = _diag_tile_exp(s_refs[mi], c * t, st["m"][mi], n_full, CHUNK,
                                         band_mask)
            else:
                pf = jnp.exp2(s_refs[mi][rows_c, :] - st["m"][mi])
                p, sums = pf.astype(BF16), _slab_reduce(pf, jnp.sum)
            p_refs[mi][rows_c, :] = p
            st["l_run"][mi] = st["l_run"][mi] + sums

        def value_task(c):
            rows_c = slice(c * t, (c + 1) * t)
            if c == 0:
                st["l1"] = jnp.sum(st["l_run"][0], axis=0, keepdims=True)
                l2 = jnp.sum(st["l_run"][1], axis=0, keepdims=True)
                st["rho"] = (lam * st["l1"] / l2).astype(BF16)
            a = p_refs[0][rows_c, :] - p_refs[1][rows_c, :] * st["rho"]
            av = _dot(vt_ref[:, rows_c], a)
            acc_ref[...] = av if c == 0 else acc_ref[...] + av
            if c == i:
                o = (acc_ref[...] / st["l1"]).T
                ms = jnp.mean(o * o, axis=-1, keepdims=True)
                y = o * lax.rsqrt(ms + GN_EPS) * g_ref[...] * (1.0 - lambda_init)
                o_ref[rows_i, :] = y.astype(o_ref.dtype)

        tiles = range(i + 1)
        return ([functools.partial(score_task, mi, c) for mi in (0, 1) for c in tiles],
                [functools.partial(exp_task, mi, c) for mi in (0, 1) for c in tiles]
                + [functools.partial(value_task, c) for c in tiles])

    _pipeline_blocks([query_block(i) for i in range(seq // t)])


def _diff_attention(z, lq1, lk1, lq2, lk2, subln_g, lambda_init, batch, seq):
    t = z.shape[1]
    width = 2 * HEAD_DIM
    col = lambda off: pl.BlockSpec((None, seq, width), lambda b, h: (off + h, b, 0))
    vec = pl.BlockSpec((1, HEAD_DIM), lambda b, h: (0, 0))
    return pl.pallas_call(
        functools.partial(_diff_kernel, lambda_init=lambda_init),
        grid=(batch, N_HEADS),
        in_specs=[
            col(0), col(N_HEADS), col(2 * N_HEADS),
            vec, vec, vec, vec,
            pl.BlockSpec((1, width), lambda b, h: (0, 0)),
        ],
        out_specs=pl.BlockSpec((seq, width), lambda b, h: (b, h)),
        out_shape=jax.ShapeDtypeStruct((t, N_HEADS * width), BF16),
        scratch_shapes=[pltpu.VMEM((width, seq), BF16),
                        pltpu.VMEM((2, seq, ATT_T), F32),
                        pltpu.VMEM((2, seq, ATT_T), F32),
                        pltpu.VMEM((seq, ATT_T), BF16),
                        pltpu.VMEM((seq, ATT_T), BF16),
                        pltpu.VMEM((width, ATT_T), F32)],
        compiler_params=_params(("parallel", "parallel"), 48),
        name="diff_attention",
    )(z, z, z, lq1, lk1, lq2, lk2, subln_g)


def _proj_residual_kernel(*refs, n_parts, n_cast):
    refs = list(refs)
    a_refs = [refs.pop(0) for _ in range(n_parts)]
    w_refs = [refs.pop(0) for _ in range(n_parts)]
    res_ref = refs.pop(0)
    cast_src = [refs.pop(0) for _ in range(n_cast)]
    o_ref = refs.pop(0)
    cast_dst = refs
    acc = res_ref[...]
    for a_ref, w_ref in zip(a_refs, w_refs):
        acc = acc + _dot(a_ref[...], w_ref[...])
    o_ref[...] = acc
    _run_cast_riders(cast_src, cast_dst)


def _proj_residual(parts, w, res, cast=()):
    t, n = res.shape
    tm = OUT_TM
    kp = parts[0].shape[1]
    n_parts = len(parts)
    in_specs = [pl.BlockSpec((tm, kp), lambda i: (i, 0)) for _ in parts]
    in_specs += [pl.BlockSpec((kp, n), functools.partial(lambda i, p: (p, 0), p=p))
                 for p in range(n_parts)]
    in_specs.append(pl.BlockSpec((tm, n), lambda i: (i, 0)))
    args = [*parts, *([w] * n_parts), res]
    out_specs = [pl.BlockSpec((tm, n), lambda i: (i, 0))]
    out_shape = [jax.ShapeDtypeStruct((t, n), F32)]
    for arr, lyr in cast:
        in_spec, out_spec, shape, _ = _cast_rider(arr, lyr, t // tm, lambda i: i)
        in_specs.append(in_spec)
        out_specs.append(out_spec)
        out_shape.append(shape)
        args.append(arr)
    out = pl.pallas_call(
        functools.partial(_proj_residual_kernel, n_parts=n_parts, n_cast=len(cast)),
        grid=(t // tm,),
        in_specs=in_specs,
        out_specs=out_specs,
        out_shape=out_shape,
        compiler_params=_params(("parallel",), 48),
        name="proj_residual",
    )(*args)
    return out if len(out) > 1 else out[0]


def _mlp_kernel(h_ref, g_ref, w1_ref, w2_ref, *refs, with_final, cast_shares, n_steps):
    refs = list(refs)
    n_cast = len(cast_shares)
    fg_ref = refs.pop(0) if with_final else None
    cast_src = [refs.pop(0) for _ in range(n_cast)]
    o_ref = refs.pop(0)
    cast_dst = [refs.pop(0) for _ in range(n_cast)]
    xn_ref, acc_ref = refs
    f = pl.program_id(1)

    @pl.when(f == 0)
    def _():
        xn_ref[...] = _rms_normalize(h_ref[...], g_ref[...]).astype(BF16)
        acc_ref[...] = jnp.zeros_like(acc_ref)

    a = jnp.maximum(_dot(xn_ref[...], w1_ref[...]), 0.0)
    acc_ref[...] += _dot((a * a).astype(BF16), w2_ref[...])
    _run_cast_riders(cast_src, cast_dst, cast_shares, pl.program_id(0) * n_steps + f)

    @pl.when(f == pl.num_programs(1) - 1)
    def _():
        out = h_ref[...] + acc_ref[...]
        if with_final:
            out = _rms_normalize(out, fg_ref[...])
        o_ref[...] = out


def _mlp(h, g, w1, w2, final_g=None, cast=()):
    t, d = h.shape
    dff = w1.shape[1]
    tm, tf = MLP_TM, MLP_TF
    with_final = final_g is not None
    n_steps = dff // tf
    in_specs = [
        pl.BlockSpec((tm, d), lambda i, f: (i, 0)),
        pl.BlockSpec((1, d), lambda i, f: (0, 0)),
        pl.BlockSpec((d, tf), lambda i, f: (0, f)),
        pl.BlockSpec((tf, d), lambda i, f: (f, 0)),
    ]
    args = [h, g, w1, w2]
    if with_final:
        in_specs.append(pl.BlockSpec((1, d), lambda i, f: (0, 0)))
        args.append(final_g)
    out_specs = [pl.BlockSpec((tm, d), lambda i, f: (i, 0))]
    out_shape = [jax.ShapeDtypeStruct((t, d), F32)]
    shares = []
    for arr, lyr in cast:
        in_spec, out_spec, shape, share = _cast_rider(arr, lyr, (t // tm) * n_steps,
                                                      lambda i, f: i * n_steps + f)
        in_specs.append(in_spec)
        out_specs.append(out_spec)
        out_shape.append(shape)
        args.append(arr)
        shares.append(share)
    out = pl.pallas_call(
        functools.partial(_mlp_kernel, with_final=with_final, cast_shares=tuple(shares),
                          n_steps=n_steps),
        grid=(t // tm, n_steps),
        in_specs=in_specs,
        out_specs=out_specs,
        out_shape=out_shape,
        scratch_shapes=[pltpu.VMEM((tm, d), BF16), pltpu.VMEM((tm, d), F32)],
        compiler_params=_params(("parallel", "arbitrary"), 56),
        name="mlp_final" if with_final else "mlp",
    )(*args)
    return out if len(out) > 1 else out[0]


def kernel(x, norm_mix_g, norm_mlp_g, even_w_in, even_b_f, even_ret_gn, even_w_out,
           odd_w_in, odd_lambda_q1, odd_lambda_k1, odd_lambda_q2, odd_lambda_k2,
           odd_subln_g, odd_w_out, mlp_w1, mlp_w2, final_g):
    batch, seq, d = x.shape
    t = batch * seq
    ret_w = N_HEADS * HEAD_DIM
    row = lambda v: v.reshape(1, -1).astype(F32)

    half = HEAD_DIM // 2
    inv = ROPE_BASE ** (-jnp.arange(half, dtype=F32) / half)
    ang = jnp.arange(seq, dtype=F32)[:, None] * inv[None, :]
    cos = jnp.concatenate([jnp.cos(ang), jnp.cos(ang)], axis=-1)
    sin_signed = jnp.concatenate([-jnp.sin(ang), jnp.sin(ang)], axis=-1)
    log_g = jnp.log1p(-(2.0 ** (-5.0 - jnp.arange(N_HEADS, dtype=F32))))

    h = x.reshape(t, d)

    n_main = 7 * ret_w
    wf = jnp.pad(even_w_in[0, :, n_main:], ((0, 0), (0, LANES - N_HEADS)))
    w_in_t = jnp.swapaxes(even_w_in, 1, 2).astype(BF16)
    z, ff, w1_0, w2_0, w_out_0 = _norm_proj(
        h, row(norm_mix_g[0]), w_in_t, 0, n_main, HEAD_DIM, wf, w_transposed=True,
        cast=[(mlp_w1, 0), (mlp_w2, 0), (even_w_out, 0)])
    b_f = jnp.pad(even_b_f[0].astype(F32), (0, LANES - N_HEADS)).reshape(1, LANES)
    fbias = _forget_bias(ff, b_f, batch, seq)
    ret = _retention(z, log_g, cos, sin_signed, row(even_ret_gn[0]), batch, seq)
    fox, w1_1, w2_1, w_in_1, w_out_1 = _fox_attention(
        z, fbias, batch, seq,
        cast=[(mlp_w1, 1), (mlp_w2, 1), (odd_w_in, 0), (odd_w_out, 0)])
    h = _proj_residual([ret, fox], w_out_0, h)
    h = _mlp(h, row(norm_mlp_g[0]), w1_0, w2_0)

    lambda_init = 0.8 - 0.6 * math.exp(-0.3 * 1)
    z = _norm_proj(h, row(norm_mix_g[1]), w_in_1[None], 0, w_in_1.shape[1], 2 * HEAD_DIM)
    att = _diff_attention(z, row(odd_lambda_q1[0]), row(odd_lambda_k1[0]),
                          row(odd_lambda_q2[0]), row(odd_lambda_k2[0]),
                          row(odd_subln_g[0]), lambda_init, batch, seq)
    h = _proj_residual([att], w_out_1, h)
    h = _mlp(h, row(norm_mlp_g[1]), w1_1, w2_1, final_g=row(final_g))
    return h.reshape(batch, seq, d)
```

```python
import functools
import math

import jax
import jax.numpy as jnp
from jax import lax
from jax.experimental import pallas as pl
from jax.experimental.pallas import tpu as pltpu

F32 = jnp.float32
BF16 = jnp.bfloat16

CHUNK = 64
CHUNK_SHIFT = 6
HEAD_DIM = 128
N_HEADS = 8
ROPE_BASE = 10000.0
RMS_EPS = 1e-6
GN_EPS = 1e-5
LOG2E = 1.4426950408889634
LANES = 128
SUBLANES = 8
BF16_ROWS = 16
MIB = 1024 * 1024

MXU_COLS = 256
PROJ_TM = 1024
PROJ_TN_MAX = 2048
OUT_TM = 512
MLP_TM = 512
MLP_TF = 1024
RET_ROWS = 256
ATT_T = 512
CUMSUM_ROWS = 256


def _params(semantics, vmem_mib):
    return pltpu.CompilerParams(dimension_semantics=semantics,
                                vmem_limit_bytes=vmem_mib * MIB)


def _dot(a, b):
    return jnp.dot(a, b, preferred_element_type=F32)


def _dot_nt(a, b):
    return lax.dot_general(a, b, (((1,), (1,)), ((), ())), preferred_element_type=F32)


def _dot_tn(a, b):
    return lax.dot_general(a, b, (((0,), (0,)), ((), ())), preferred_element_type=F32)


def _rms_normalize(x, g):
    ms = jnp.mean(x * x, axis=-1, keepdims=True)
    return (x * lax.rsqrt(ms + RMS_EPS)) * g


def _slab_reduce(x, op):
    rows, n = x.shape
    return op(x.reshape(rows // SUBLANES, SUBLANES, n), axis=0)


def _cast_rider(arr, layer, n_steps, step_index):
    rows, cols = arr.shape[1:]
    slab = max(BF16_ROWS, rows // n_steps)
    share = slab * n_steps // rows
    in_spec = pl.BlockSpec((None, slab, cols),
                           lambda *ids: (layer, step_index(*ids) // share, 0))
    out_spec = pl.BlockSpec((slab, cols), lambda *ids: (step_index(*ids) // share, 0))
    return in_spec, out_spec, jax.ShapeDtypeStruct((rows, cols), BF16), share


def _run_cast_riders(src_refs, dst_refs, shares=None, step=None):
    for k, (src, dst) in enumerate(zip(src_refs, dst_refs)):
        def cast(src=src, dst=dst):
            dst[...] = src[...].astype(dst.dtype)
        if shares is None or shares[k] == 1:
            cast()
        else:
            pl.when(lax.rem(step, shares[k]) == 0)(cast)


def _norm_proj_kernel(*refs, with_forget, w_transposed, n_cast):
    x_ref, g_ref, w_ref = refs[:3]
    refs = list(refs[3:])
    wf_ref = refs.pop(0) if with_forget else None
    cast_src = [refs.pop(0) for _ in range(n_cast)]
    o_ref = refs.pop(0)
    f_ref = refs.pop(0) if with_forget else None
    cast_dst = [refs.pop(0) for _ in range(n_cast)]
    (xn_ref,) = refs

    @pl.when(pl.program_id(1) == 0)
    def _():
        xn_ref[...] = _rms_normalize(x_ref[...], g_ref[...]).astype(BF16)
        if with_forget:
            f_ref[...] = _dot(xn_ref[...], wf_ref[...].astype(BF16))

    matmul = _dot_nt if w_transposed else _dot
    res = matmul(xn_ref[...], w_ref[...]).astype(o_ref.dtype)
    group = o_ref.shape[2]
    for c in range(o_ref.shape[0]):
        o_ref[c] = res[:, c * group:(c + 1) * group]
    _run_cast_riders(cast_src, cast_dst)


def _norm_proj(x, g, w, layer, n, group, wf=None, w_transposed=False, cast=()):
    t, d = x.shape
    tm = PROJ_TM
    tn = max(c for c in range(MXU_COLS, PROJ_TN_MAX + 1, MXU_COLS) if n % c == 0)
    with_forget = wf is not None
    n_col_steps = n // tn
    riders = [_cast_rider(arr, lyr, (t // tm) * n_col_steps, lambda i, j: i * n_col_steps + j)
              for arr, lyr in cast]
    if w_transposed:
        w_spec = pl.BlockSpec((None, tn, d), lambda i, j: (layer, j, 0))
    else:
        w_spec = pl.BlockSpec((None, d, tn), lambda i, j: (layer, 0, j))
    in_specs = [
        pl.BlockSpec((tm, d), lambda i, j: (i, 0)),
        pl.BlockSpec((1, d), lambda i, j: (0, 0)),
        w_spec,
    ]
    out_shape = [jax.ShapeDtypeStruct((n // group, t, group), BF16)]
    out_specs = [pl.BlockSpec((tn // group, tm, group), lambda i, j: (j, i, 0))]
    args = [x, g, w]
    if with_forget:
        in_specs.append(pl.BlockSpec((d, LANES), lambda i, j: (0, 0)))
        out_shape.append(jax.ShapeDtypeStruct((t, LANES), F32))
        out_specs.append(pl.BlockSpec((tm, LANES), lambda i, j: (i, 0)))
        args.append(wf)
    for (in_spec, out_spec, shape, _), (arr, _) in zip(riders, cast):
        in_specs.append(in_spec)
        out_specs.append(out_spec)
        out_shape.append(shape)
        args.append(arr)
    out = pl.pallas_call(
        functools.partial(_norm_proj_kernel, with_forget=with_forget,
                          w_transposed=w_transposed, n_cast=len(cast)),
        grid=(t // tm, n_col_steps),
        in_specs=in_specs,
        out_specs=out_specs,
        out_shape=out_shape,
        scratch_shapes=[pltpu.VMEM((tm, d), BF16)],
        compiler_params=_params(("parallel", "arbitrary"), 56),
        name="norm_proj_forget" if with_forget else "norm_proj",
    )(*args)
    return out if len(out) > 1 else out[0]


def _rotate(x, cos, sin_signed):
    return x * cos + pltpu.roll(x, HEAD_DIM // 2, 1) * sin_signed


def _retention_kernel(lg_ref, q_ref, k_ref, v_ref, gate_ref, cos_ref, sin_ref, gn_ref, o_ref):
    seq = q_ref.shape[0]
    rows = RET_ROWS
    lg = lg_ref[pl.program_id(1)]

    row = lax.broadcasted_iota(jnp.int32, (rows, HEAD_DIM), 0).astype(F32)
    xi = jnp.exp(lg * (row + 1.0))
    zeta = jnp.exp(lg * (rows - 1.0 - row))
    ti = lax.broadcasted_iota(jnp.int32, (rows, rows), 0)
    si = lax.broadcasted_iota(jnp.int32, (rows, rows), 1)
    visible = (si >> CHUNK_SHIFT) <= (ti >> CHUNK_SHIFT)
    decay = jnp.where(visible, jnp.exp(lg * jnp.abs(ti - si).astype(F32)), 0.0)
    block_decay = jnp.exp(jnp.full((1, HEAD_DIM), lg * rows, F32))
    gn = gn_ref[...]

    state = jnp.zeros((HEAD_DIM, HEAD_DIM), F32)
    for r in range(seq // rows):
        sl = slice(r * rows, (r + 1) * rows)
        cos = cos_ref[sl, :]
        sin = sin_ref[sl, :]
        q = _rotate(q_ref[sl, :].astype(F32), cos, sin) * (HEAD_DIM ** -0.5)
        k = _rotate(k_ref[sl, :].astype(F32), cos, sin)
        v = v_ref[sl, :]
        scores = _dot_nt(q.astype(BF16), k.astype(BF16)) * decay
        out = _dot(scores.astype(BF16), v) + _dot((q * xi).astype(BF16), state.astype(BF16))
        state = state * block_decay + _dot_tn((k * zeta).astype(BF16), v)

        ms = jnp.mean(out * out, axis=-1, keepdims=True)
        y = out * lax.rsqrt(ms + GN_EPS) * gn
        gate = gate_ref[sl, :].astype(F32)
        o_ref[sl, :] = (jax.nn.silu(gate) * y).astype(o_ref.dtype)


def _retention(z, log_g, cos, sin_signed, gn, batch, seq):
    t = z.shape[1]
    col = lambda off: pl.BlockSpec((None, seq, HEAD_DIM), lambda b, h: (off + h, b, 0))
    tab = pl.BlockSpec((seq, HEAD_DIM), lambda b, h: (0, 0))
    return pl.pallas_call(
        _retention_kernel,
        grid=(batch, N_HEADS),
        in_specs=[
            pl.BlockSpec(memory_space=pltpu.SMEM),
            col(0), col(N_HEADS), col(2 * N_HEADS), col(3 * N_HEADS),
            tab, tab,
            pl.BlockSpec((1, HEAD_DIM), lambda b, h: (0, h)),
        ],
        out_specs=pl.BlockSpec((seq, HEAD_DIM), lambda b, h: (b, h)),
        out_shape=jax.ShapeDtypeStruct((t, N_HEADS * HEAD_DIM), BF16),
        compiler_params=_params(("parallel", "parallel"), 32),
        name="retention",
    )(log_g, z, z, z, z, cos, sin_signed, gn)


def _forget_bias_kernel(ff_ref, b_ref, tri_ref, o_ref):
    seq = ff_ref.shape[0]
    rows = tri_ref.shape[0]
    lane = lax.broadcasted_iota(jnp.int32, (rows, LANES), 1)
    carry = jnp.zeros((1, LANES), F32)
    for blk in range(seq // rows):
        sl = slice(blk * rows, (blk + 1) * rows)
        log_f = jax.nn.log_sigmoid(ff_ref[sl, :] + b_ref[...])
        c = jnp.dot(tri_ref[...], log_f, precision=lax.Precision.HIGHEST,
                    preferred_element_type=F32) + carry
        carry = c[rows - 1:rows, :]
        for h in range(N_HEADS):
            x = jnp.broadcast_to(c[:, h:h + 1], (rows, LANES)) * (-LOG2E)
            hi = x.astype(BF16).astype(F32)
            mid = (x - hi).astype(BF16).astype(F32)
            lo = x - hi - mid
            pieces = jnp.where(lane == 0, hi, jnp.where(lane == 1, mid,
                                                       jnp.where(lane == 2, lo, 0.0)))
            o_ref[h, sl, :] = pieces.astype(o_ref.dtype)


def _forget_bias(ff, b_f, batch, seq):
    rows = CUMSUM_ROWS
    tri = (jnp.arange(rows)[None, :] <= jnp.arange(rows)[:, None]).astype(F32)
    out = pl.pallas_call(
        _forget_bias_kernel,
        grid=(batch,),
        in_specs=[
            pl.BlockSpec((seq, LANES), lambda b: (b, 0)),
            pl.BlockSpec((1, LANES), lambda b: (0, 0)),
            pl.BlockSpec((rows, rows), lambda b: (0, 0)),
        ],
        out_specs=pl.BlockSpec((None, N_HEADS, seq, LANES), lambda b: (b, 0, 0, 0)),
        out_shape=jax.ShapeDtypeStruct((batch, N_HEADS, seq, LANES), BF16),
        compiler_params=_params(("parallel",), 32),
        name="forget_bias",
    )(ff, b_f, tri)
    return out.reshape(batch * N_HEADS * seq, LANES)


def _pipeline_blocks(blocks):
    for task in blocks[0][0]:
        task()
    for n, (_, value_tasks) in enumerate(blocks):
        score_tasks = blocks[n + 1][0] if n + 1 < len(blocks) else []
        done = 0
        for idx, task in enumerate(value_tasks):
            task()
            while done < len(score_tasks) and done * len(value_tasks) < (idx + 1) * len(score_tasks):
                score_tasks[done]()
                done += 1


def _diag_tile_max(s, n_full, band_rows, band_mask):
    parts = []
    for j in range(s.shape[1] // LANES):
        lanes = slice(j * LANES, (j + 1) * LANES)
        nf = n_full(j)
        band = jnp.where(band_mask, s[nf:nf + band_rows, lanes], -jnp.inf)
        mx = _slab_reduce(band, jnp.max)
        if nf:
            mx = jnp.maximum(mx, _slab_reduce(s[:nf, lanes], jnp.max))
        parts.append(mx)
    return jnp.concatenate(parts, axis=1)


def _diag_tile_exp(s_ref, row0, m, n_full, band_rows, band_mask):
    t = s_ref.shape[1]
    strips, sums = [], []
    for j in range(t // LANES):
        lanes = slice(j * LANES, (j + 1) * LANES)
        nf = n_full(j)
        nv = nf + band_rows
        pj = jnp.exp2(s_ref[row0:row0 + nv, lanes] - m[:, lanes])
        band = jnp.where(band_mask, pj[nf:, :], 0.0)
        total = _slab_reduce(band, jnp.sum)
        pieces = [band.astype(BF16)]
        if nf:
            total = total + _slab_reduce(pj[:nf, :], jnp.sum)
            pieces.insert(0, pj[:nf, :].astype(BF16))
        if nv < t:
            pieces.append(jnp.zeros((t - nv, LANES), BF16))
        strips.append(jnp.concatenate(pieces, axis=0))
        sums.append(total)
    return jnp.concatenate(strips, axis=1), jnp.concatenate(sums, axis=1)


def _fox_kernel(q_ref, k_ref, v_ref, fb_ref, *refs, n_cast):
    refs = list(refs)
    cast_src = [refs.pop(0) for _ in range(n_cast)]
    o_ref = refs.pop(0)
    cast_dst = [refs.pop(0) for _ in range(n_cast)]
    vt_ref, s_ref = refs
    seq = q_ref.shape[0]
    t = ATT_T
    _run_cast_riders(cast_src, cast_dst)
    vt_ref[...] = v_ref[...].astype(F32).T.astype(BF16)
    lane = lax.broadcasted_iota(jnp.int32, (t, HEAD_DIM), 1)
    ones3 = jnp.where(lane < 3, 1.0, 0.0).astype(BF16)
    band_mask = (lax.broadcasted_iota(jnp.int32, (LANES, LANES), 0)
                 <= lax.broadcasted_iota(jnp.int32, (LANES, LANES), 1))
    n_full = lambda j: j * LANES
    qscale = (HEAD_DIM ** -0.5) * LOG2E

    def query_block(i):
        rows_i = slice(i * t, (i + 1) * t)
        sbuf = s_ref.at[i % 2]
        st = {"m_run": None, "l_run": jnp.zeros((SUBLANES, t), F32),
              "acc": jnp.zeros((HEAD_DIM, t), F32)}

        def score_task(c):
            rows_c = slice(c * t, (c + 1) * t)
            if c == 0:
                qs = (q_ref[rows_i, :].astype(F32) * qscale).astype(BF16)
                st["qa"] = jnp.concatenate([qs, ones3], axis=1)
            ka = jnp.concatenate([k_ref[rows_c, :], fb_ref[rows_c, :]], axis=1)
            s = _dot_nt(ka, st["qa"])
            sbuf[rows_c, :] = s
            if c == i:
                tile_max = _diag_tile_max(s, n_full, LANES, band_mask)
            else:
                tile_max = _slab_reduce(s, jnp.max)
            st["m_run"] = tile_max if c == 0 else jnp.maximum(st["m_run"], tile_max)

        def value_task(c):
            rows_c = slice(c * t, (c + 1) * t)
            if c == 0:
                st["m"] = jnp.max(st["m_run"], axis=0, keepdims=True)
            if c == i:
                p, sums = _diag_tile_exp(sbuf, c * t, st["m"], n_full, LANES, band_mask)
            else:
                pf = jnp.exp2(sbuf[rows_c, :] - st["m"])
                p, sums = pf.astype(BF16), _slab_reduce(pf, jnp.sum)
            st["l_run"] = st["l_run"] + sums
            st["acc"] = st["acc"] + _dot(vt_ref[:, rows_c], p)
            if c == i:
                l = jnp.sum(st["l_run"], axis=0, keepdims=True)
                o_ref[rows_i, :] = (st["acc"] / l).T.astype(o_ref.dtype)

        tiles = range(i + 1)
        return ([functools.partial(score_task, c) for c in tiles],
                [functools.partial(value_task, c) for c in tiles])

    _pipeline_blocks([query_block(i) for i in range(seq // t)])


def _fox_attention(z, fbias, batch, seq, cast=()):
    t = z.shape[1]
    col = lambda off: pl.BlockSpec((None, seq, HEAD_DIM), lambda b, h: (off + h, b, 0))
    in_specs = [
        col(4 * N_HEADS), col(5 * N_HEADS), col(6 * N_HEADS),
        pl.BlockSpec((seq, LANES), lambda b, h: (b * N_HEADS + h, 0)),
    ]
    args = [z, z, z, fbias]
    out_specs = [pl.BlockSpec((seq, HEAD_DIM), lambda b, h: (b, h))]
    out_shape = [jax.ShapeDtypeStruct((t, N_HEADS * HEAD_DIM), BF16)]
    for arr, lyr in cast:
        in_spec, out_spec, shape, _ = _cast_rider(arr, lyr, batch * N_HEADS,
                                                  lambda b, h: b * N_HEADS + h)
        in_specs.append(in_spec)
        out_specs.append(out_spec)
        out_shape.append(shape)
        args.append(arr)
    out = pl.pallas_call(
        functools.partial(_fox_kernel, n_cast=len(cast)),
        grid=(batch, N_HEADS),
        in_specs=in_specs,
        out_specs=out_specs,
        out_shape=out_shape,
        scratch_shapes=[pltpu.VMEM((HEAD_DIM, seq), BF16),
                        pltpu.VMEM((2, seq, ATT_T), F32)],
        compiler_params=_params(("parallel", "parallel"), 48),
        name="fox_attention",
    )(*args)
    return out if len(out) > 1 else out[0]


def _diff_kernel(q_ref, k_ref, v_ref, lq1_ref, lk1_ref, lq2_ref, lk2_ref, g_ref, o_ref,
                 vt_ref, s1_ref, s2_ref, p1_ref, p2_ref, acc_ref, *, lambda_init):
    seq = q_ref.shape[0]
    t = ATT_T
    vt_ref[...] = v_ref[...].astype(F32).T.astype(BF16)
    lam = (jnp.exp(jnp.sum(lq1_ref[...] * lk1_ref[...], axis=-1, keepdims=True))
           - jnp.exp(jnp.sum(lq2_ref[...] * lk2_ref[...], axis=-1, keepdims=True))
           + lambda_init)
    band_mask = lax.broadcasted_iota(jnp.int32, (CHUNK, LANES), 1) >= CHUNK
    n_full = lambda j: j * LANES + CHUNK
    qscale = (HEAD_DIM ** -0.5) * LOG2E
    col_slices = (slice(0, HEAD_DIM), slice(HEAD_DIM, 2 * HEAD_DIM))
    def query_block(i):
        rows_i = slice(i * t, (i + 1) * t)
        s_refs = (s1_ref.at[i % 2], s2_ref.at[i % 2])
        p_refs = (p1_ref, p2_ref)
        zero = jnp.zeros((SUBLANES, t), F32)
        st = {"m_run": [None, None], "m": [None, None], "l_run": [zero, zero]}

        def score_task(mi, c):
            rows_c = slice(c * t, (c + 1) * t)
            if mi == 0 and c == 0:
                st["qs"] = (q_ref[rows_i, :].astype(F32) * qscale).astype(BF16)
            cols = col_slices[mi]
            s = _dot_nt(k_ref[rows_c, cols], st["qs"][:, cols])
            s_refs[mi][rows_c, :] = s
            if c == i:
                tile_max = _diag_tile_max(s, n_full, CHUNK, band_mask)
            else:
                tile_max = _slab_reduce(s, jnp.max)
            st["m_run"][mi] = tile_max if c == 0 else jnp.maximum(st["m_run"][mi], tile_max)

        def exp_task(mi, c):
            rows_c = slice(c * t, (c + 1) * t)
            if c == 0:
                st["m"][mi] = jnp.max(st["m_run"][mi], axis=0, keepdims=True)
            if c == i:
                p, sums = _diag_tile_exp(s_refs[mi], c * t, st["m"][mi], n_full, CHUNK,
                                         band_mask)
            else:
                pf = jnp.exp2(s_refs[mi][rows_c, :] - st["m"][mi])
                p, sums = pf.astype(BF16), _slab_reduce(pf, jnp.sum)
            p_refs[mi][rows_c, :] = p
            st["l_run"][mi] = st["l_run"][mi] + sums

        def value_task(c):
            rows_c = slice(c * t, (c + 1) * t)
            if c == 0:
                st["l1"] = jnp.sum(st["l_run"][0], axis=0, keepdims=True)
                l2 = jnp.sum(st["l_run"][1], axis=0, keepdims=True)
                st["rho"] = (lam * st["l1"] / l2).astype(BF16)
            a = p_refs[0][rows_c, :] - p_refs[1][rows_c, :] * st["rho"]
            av = _dot(vt_ref[:, rows_c], a)
            acc_ref[...] = av if c == 0 else acc_ref[...] + av
            if c == i:
                o = (acc_ref[...] / st["l1"]).T
                ms = jnp.mean(o * o, axis=-1, keepdims=True)
                y = o * lax.rsqrt(ms + GN_EPS) * g_ref[...] * (1.0 - lambda_init)
                o_ref[rows_i, :] = y.astype(o_ref.dtype)

        tiles = range(i + 1)
        return ([functools.partial(score_task, mi, c) for mi in (0, 1) for c in tiles],
                [functools.partial(exp_task, mi, c) for mi in (0, 1) for c in tiles]
                + [functools.partial(value_task, c) for c in tiles])

    _pipeline_blocks([query_block(i) for i in range(seq // t)])


def _diff_attention(z, lq1, lk1, lq2, lk2, subln_g, lambda_init, batch, seq):
    t = z.shape[1]
    width = 2 * HEAD_DIM
    col = lambda off: pl.BlockSpec((None, seq, width), lambda b, h: (off + h, b, 0))
    vec = pl.BlockSpec((1, HEAD_DIM), lambda b, h: (0, 0))
    return pl.pallas_call(
        functools.partial(_diff_kernel, lambda_init=lambda_init),
        grid=(batch, N_HEADS),
        in_specs=[
            col(0), col(N_HEADS), col(2 * N_HEADS),
            vec, vec, vec, vec,
            pl.BlockSpec((1, width), lambda b, h: (0, 0)),
        ],
        out_specs=pl.BlockSpec((seq, width), lambda b, h: (b, h)),
        out_shape=jax.ShapeDtypeStruct((t, N_HEADS * width), BF16),
        scratch_shapes=[pltpu.VMEM((width, seq), BF16),
                        pltpu.VMEM((2, seq, ATT_T), F32),
                        pltpu.VMEM((2, seq, ATT_T), F32),
                        pltpu.VMEM((seq, ATT_T), BF16),
                        pltpu.VMEM((seq, ATT_T), BF16),
                        pltpu.VMEM((width, ATT_T), F32)],
        compiler_params=_params(("parallel", "parallel"), 48),
        name="diff_attention",
    )(z, z, z, lq1, lk1, lq2, lk2, subln_g)


def _proj_residual_kernel(*refs, n_parts, n_cast):
    refs = list(refs)
    a_refs = [refs.pop(0) for _ in range(n_parts)]
    w_refs = [refs.pop(0) for _ in range(n_parts)]
    res_ref = refs.pop(0)
    cast_src = [refs.pop(0) for _ in range(n_cast)]
    o_ref = refs.pop(0)
    cast_dst = refs
    acc = res_ref[...]
    for a_ref, w_ref in zip(a_refs, w_refs):
        acc = acc + _dot(a_ref[...], w_ref[...])
    o_ref[...] = acc
    _run_cast_riders(cast_src, cast_dst)


def _proj_residual(parts, w, res, cast=()):
    t, n = res.shape
    tm = OUT_TM
    kp = parts[0].shape[1]
    n_parts = len(parts)
    in_specs = [pl.BlockSpec((tm, kp), lambda i: (i, 0)) for _ in parts]
    in_specs += [pl.BlockSpec((kp, n), functools.partial(lambda i, p: (p, 0), p=p))
                 for p in range(n_parts)]
    in_specs.append(pl.BlockSpec((tm, n), lambda i: (i, 0)))
    args = [*parts, *([w] * n_parts), res]
    out_specs = [pl.BlockSpec((tm, n), lambda i: (i, 0))]
    out_shape = [jax.ShapeDtypeStruct((t, n), F32)]
    for arr, lyr in cast:
        in_spec, out_spec, shape, _ = _cast_rider(arr, lyr, t // tm, lambda i: i)
        in_specs.append(in_spec)
        out_specs.append(out_spec)
        out_shape.append(shape)
        args.append(arr)
    out = pl.pallas_call(
        functools.partial(_proj_residual_kernel, n_parts=n_parts, n_cast=len(cast)),
        grid=(t // tm,),
        in_specs=in_specs,
        out_specs=out_specs,
        out_shape=out_shape,
        compiler_params=_params(("parallel",), 48),
        name="proj_residual",
    )(*args)
    return out if len(out) > 1 else out[0]


def _mlp_kernel(h_ref, g_ref, w1_ref, w2_ref, *refs, with_final, cast_shares, n_steps):
    refs = list(refs)
    n_cast = len(cast_shares)
    fg_ref = refs.pop(0) if with_final else None
    cast_src = [refs.pop(0) for _ in range(n_cast)]
    o_ref = refs.pop(0)
    cast_dst = [refs.pop(0) for _ in range(n_cast)]
    xn_ref, acc_ref = refs
    f = pl.program_id(1)

    @pl.when(f == 0)
    def _():
        xn_ref[...] = _rms_normalize(h_ref[...], g_ref[...]).astype(BF16)
        acc_ref[...] = jnp.zeros_like(acc_ref)

    a = jnp.maximum(_dot(xn_ref[...], w1_ref[...]), 0.0)
    acc_ref[...] += _dot((a * a).astype(BF16), w2_ref[...])
    _run_cast_riders(cast_src, cast_dst, cast_shares, pl.program_id(0) * n_steps + f)

    @pl.when(f == pl.num_programs(1) - 1)
    def _():
        out = h_ref[...] + acc_ref[...]
        if with_final:
            out = _rms_normalize(out, fg_ref[...])
        o_ref[...] = out


def _mlp(h, g, w1, w2, final_g=None, cast=()):
    t, d = h.shape
    dff = w1.shape[1]
    tm, tf = MLP_TM, MLP_TF
    with_final = final_g is not None
    n_steps = dff // tf
    in_specs = [
        pl.BlockSpec((tm, d), lambda i, f: (i, 0)),
        pl.BlockSpec((1, d), lambda i, f: (0, 0)),
        pl.BlockSpec((d, tf), lambda i, f: (0, f)),
        pl.BlockSpec((tf, d), lambda i, f: (f, 0)),
    ]
    args = [h, g, w1, w2]
    if with_final:
        in_specs.append(pl.BlockSpec((1, d), lambda i, f: (0, 0)))
        args.append(final_g)
    out_specs = [pl.BlockSpec((tm, d), lambda i, f: (i, 0))]
    out_shape = [jax.ShapeDtypeStruct((t, d), F32)]
    shares = []
    for arr, lyr in cast:
        in_spec, out_spec, shape, share = _cast_rider(arr, lyr, (t // tm) * n_steps,
                                                      lambda i, f: i * n_steps + f)
        in_specs.append(in_spec)
        out_specs.append(out_spec)
        out_shape.append(shape)
        args.append(arr)
        shares.append(share)
    out = pl.pallas_call(
        functools.partial(_mlp_kernel, with_final=with_final, cast_shares=tuple(shares),
                          n_steps=n_steps),
        grid=(t // tm, n_steps),
        in_specs=in_specs,
        out_specs=out_specs,
        out_shape=out_shape,
        scratch_shapes=[pltpu.VMEM((tm, d), BF16), pltpu.VMEM((tm, d), F32)],
        compiler_params=_params(("parallel", "arbitrary"), 56),
        name="mlp_final" if with_final else "mlp",
    )(*args)
    return out if len(out) > 1 else out[0]


def kernel(x, norm_mix_g, norm_mlp_g, even_w_in, even_b_f, even_ret_gn, even_w_out,
           odd_w_in, odd_lambda_q1, odd_lambda_k1, odd_lambda_q2, odd_lambda_k2,
           odd_subln_g, odd_w_out, mlp_w1, mlp_w2, final_g):
    batch, seq, d = x.shape
    t = batch * seq
    ret_w = N_HEADS * HEAD_DIM
    row = lambda v: v.reshape(1, -1).astype(F32)

    half = HEAD_DIM // 2
    inv = ROPE_BASE ** (-jnp.arange(half, dtype=F32) / half)
    ang = jnp.arange(seq, dtype=F32)[:, None] * inv[None, :]
    cos = jnp.concatenate([jnp.cos(ang), jnp.cos(ang)], axis=-1)
    sin_signed = jnp.concatenate([-jnp.sin(ang), jnp.sin(ang)], axis=-1)
    log_g = jnp.log1p(-(2.0 ** (-5.0 - jnp.arange(N_HEADS, dtype=F32))))

    h = x.reshape(t, d)

    n_main = 7 * ret_w
    wf = jnp.pad(even_w_in[0, :, n_main:], ((0, 0), (0, LANES - N_HEADS)))
    w_in_t = jnp.swapaxes(even_w_in, 1, 2).astype(BF16)
    z, ff = _norm_proj(h, row(norm_mix_g[0]), w_in_t, 0, n_main, HEAD_DIM, wf,
                       w_transposed=True)
    b_f = jnp.pad(even_b_f[0].astype(F32), (0, LANES - N_HEADS)).reshape(1, LANES)
    fbias = _forget_bias(ff, b_f, batch, seq)
    ret = _retention(z, log_g, cos, sin_signed, row(even_ret_gn[0]), batch, seq)
    fox, w1_0, w2_0, w_out_0, w1_1, w2_1, w_in_1, w_out_1 = _fox_attention(
        z, fbias, batch, seq,
        cast=[(mlp_w1, 0), (mlp_w2, 0), (even_w_out, 0),
              (mlp_w1, 1), (mlp_w2, 1), (odd_w_in, 0), (odd_w_out, 0)])
    h = _proj_residual([ret, fox], w_out_0, h)
    h = _mlp(h, row(norm_mlp_g[0]), w1_0, w2_0)

    lambda_init = 0.8 - 0.6 * math.exp(-0.3 * 1)
    z = _norm_proj(h, row(norm_mix_g[1]), w_in_1[None], 0, w_in_1.shape[1], 2 * HEAD_DIM)
    att = _diff_attention(z, row(odd_lambda_q1[0]), row(odd_lambda_k1[0]),
                          row(odd_lambda_q2[0]), row(odd_lambda_k2[0]),
                          row(odd_subln_g[0]), lambda_init, batch, seq)
    h = _proj_residual([att], w_out_1, h)
    h = _mlp(h, row(norm_mlp_g[1]), w1_1, w2_1, final_g=row(final_g))
    return h.reshape(batch, seq, d)
```
